```python
import math
import jax, jax.numpy as jnp
from jax import lax
import numpy as np

D_MODEL = 1024
BATCH = 8
SEQ = 8192
DEPTH = 1

N_HEADS = 8
HEAD_DIM = 64
V_HEAD_DIM = 2 * HEAD_DIM
ATTN_QK_WIDTH = 2 * N_HEADS * HEAD_DIM
ATTN_V_WIDTH = N_HEADS * V_HEAD_DIM
Q_BLOCK = 128
POOL_WINDOWS = (2, 4, 8, 16)
POOL_GROUPS = len(POOL_WINDOWS)
POOL_GROUP_DIM = 128
POOL_WIDTH = POOL_GROUPS * POOL_GROUP_DIM
N_BRANCHES = 2
IN_WIDTH = 2 * ATTN_QK_WIDTH + ATTN_V_WIDTH + POOL_WIDTH + N_BRANCHES * D_MODEL
PEER_HEADS = 8
PEER_KEYS = 128
PEER_EXPERTS = PEER_KEYS * PEER_KEYS
PEER_QUERY_DIM = 256
PEER_HALF = PEER_QUERY_DIM // 2
PEER_TOPK = 16
PEER_CHUNK = 128
DN_ALPHA = (2.0 * DEPTH) ** 0.25
DN_BETA = (8.0 * DEPTH) ** -0.25
LN_EPS = 1e-5

kernel_name = "hybrid_diffattn_pool_peer_deepnorm"


def layer_norm(x, g, b):
    xf = x.astype(jnp.float32)
    mu = jnp.mean(xf, axis=-1, keepdims=True)
    var = jnp.mean(jnp.square(xf - mu), axis=-1, keepdims=True)
    return ((xf - mu) * lax.rsqrt(var + LN_EPS) * g.astype(jnp.float32) + b.astype(jnp.float32)).astype(x.dtype)


def rms_norm(x, g):
    xf = x.astype(jnp.float32)
    return (xf * lax.rsqrt(jnp.mean(xf * xf, axis=-1, keepdims=True) + LN_EPS) * g.astype(jnp.float32)).astype(x.dtype)


def alibi_slopes(n):
    return jnp.asarray(np.array([2.0 ** (-8.0 * (h + 1) / n) for h in range(n)], dtype=np.float32))


def diff_attention(q1, q2, k1, k2, v, lam, slopes):
    S = q1.shape[2]
    scale = HEAD_DIM ** -0.5
    outs = []
    for blk in range(S // Q_BLOCK):
        lo, hi = blk * Q_BLOCK, (blk + 1) * Q_BLOCK
        dist = (jnp.arange(lo, hi)[:, None] - jnp.arange(hi)[None, :]).astype(jnp.float32)
        bias = jnp.where(dist[None] >= 0, -slopes[:, None, None] * dist[None], -jnp.inf)

        def probs(q, k):
            s = jnp.einsum('bhqd,bhkd->bhqk', q[:, :, lo:hi], k[:, :, :hi]).astype(jnp.float32) * scale
            return jax.nn.softmax(s + bias[None], axis=-1)

        a = probs(q1, k1) - lam * probs(q2, k2)
        outs.append(jnp.einsum('bhqk,bhkd->bhqd', a.astype(v.dtype), v[:, :, :hi]))
    return jnp.concatenate(outs, axis=2)


def multiscale_pool(p):
    S = p.shape[1]
    pf = p.astype(jnp.float32)
    cs = jnp.cumsum(pf, axis=1)
    wmax = max(POOL_WINDOWS)
    cs_pad = jnp.pad(cs, ((0, 0), (wmax, 0), (0, 0)))
    t = jnp.arange(S, dtype=jnp.float32)[None, :, None]
    groups = []
    for gi, w in enumerate(POOL_WINDOWS):
        sl = slice(gi * POOL_GROUP_DIM, (gi + 1) * POOL_GROUP_DIM)
        lagged = cs_pad[:, wmax - w: wmax - w + S, sl]
        cnt = jnp.minimum(float(w), t + 1.0)
        groups.append((cs[:, :, sl] - lagged) / cnt - pf[:, :, sl])
    return jnp.stack(groups, axis=2)


def hybrid_mixer(x, w_in, b_gate, lq1, lk1, lq2, lk2, subln_g, w_attn_proj,
                 pool_w, pool_scale, w_pool_proj, w_out, lam_init):
    B, S, _ = x.shape
    h = x @ w_in
    q, k, v, p, g = jnp.split(h, [ATTN_QK_WIDTH, 2 * ATTN_QK_WIDTH, 2 * ATTN_QK_WIDTH + ATTN_V_WIDTH,
                                  2 * ATTN_QK_WIDTH + ATTN_V_WIDTH + POOL_WIDTH], axis=-1)
    q = q.reshape(B, S, N_HEADS, 2, HEAD_DIM).transpose(3, 0, 2, 1, 4)
    k = k.reshape(B, S, N_HEADS, 2, HEAD_DIM).transpose(3, 0, 2, 1, 4)
    v = v.reshape(B, S, N_HEADS, V_HEAD_DIM).transpose(0, 2, 1, 3)
    f32 = jnp.float32
    lam = (jnp.exp(jnp.sum(lq1.astype(f32) * lk1.astype(f32)))
           - jnp.exp(jnp.sum(lq2.astype(f32) * lk2.astype(f32))) + lam_init)
    o = diff_attention(q[0], q[1], k[0], k[1], v, lam, alibi_slopes(N_HEADS))
    o = rms_norm(o, subln_g) * (1.0 - lam_init)
    o = o.transpose(0, 2, 1, 3).reshape(B, S, ATTN_V_WIDTH)
    attn_branch = o @ w_attn_proj
    pooled = multiscale_pool(p).astype(x.dtype)
    pm = jnp.einsum('bsgc,gcd->bsgd', pooled, pool_w).reshape(B, S, POOL_WIDTH) * pool_scale
    pool_branch = pm @ w_pool_proj
    gates = jax.nn.sigmoid((g + b_gate).astype(f32)).astype(x.dtype)
    g_attn, g_pool = jnp.split(gates, N_BRANCHES, axis=-1)
    merged = g_attn * attn_branch + g_pool * pool_branch
    return merged @ w_out


def peer_ffn(y, w_q, sub_keys, u, v):
    B, S, D = y.shape
    T = B * S
    yc = y.reshape(T // PEER_CHUNK, PEER_CHUNK, D)

    def chunk(yb):
        q = (yb @ w_q).reshape(PEER_CHUNK, PEER_HEADS, 2, PEER_HALF)
        s = jnp.einsum('chpk,hpnk->chpn', q, sub_keys).astype(jnp.float32)
        s1, i1 = lax.top_k(s[:, :, 0], PEER_TOPK)
        s2, i2 = lax.top_k(s[:, :, 1], PEER_TOPK)
        cand = (s1[..., :, None] + s2[..., None, :]).reshape(PEER_CHUNK, PEER_HEADS, PEER_TOPK * PEER_TOPK)
        sc, ci = lax.top_k(cand, PEER_TOPK)
        row = jnp.take_along_axis(i1, ci // PEER_TOPK, axis=-1)
        col = jnp.take_along_axis(i2, ci % PEER_TOPK, axis=-1)
        idx = row * PEER_KEYS + col
        gate = jax.nn.softmax(sc, axis=-1)
        act = jax.nn.gelu(jnp.einsum('ched,cd->che', u[idx], yb), approximate=False)
        return jnp.einsum('che,ched->cd', (gate * act).astype(yb.dtype), v[idx])

    return lax.map(chunk, yc).reshape(B, S, D)


def setup_inputs(seed: int = 0) -> dict:
    key = jax.random.key(seed)
    ks = jax.random.split(key, 22)
    n = lambda k, shape, s: jax.random.normal(k, shape, jnp.float32) * s
    L, D = DEPTH, D_MODEL
    return {
        "x": n(ks[0], (BATCH, SEQ, D), 1.0),
        "w_in": n(ks[1], (L, D, IN_WIDTH), D ** -0.5),
        "b_gate": n(ks[2], (L, N_BRANCHES * D), 0.01),
        "lambda_q1": n(ks[3], (L, HEAD_DIM), 0.1),
        "lambda_k1": n(ks[4], (L, HEAD_DIM), 0.1),
        "lambda_q2": n(ks[5], (L, HEAD_DIM), 0.1),
        "lambda_k2": n(ks[6], (L, HEAD_DIM), 0.1),
        "subln_g": 1.0 + n(ks[7], (L, V_HEAD_DIM), 0.02),
        "w_attn_proj": n(ks[8], (L, ATTN_V_WIDTH, D), ATTN_V_WIDTH ** -0.5 * DN_BETA),
        "pool_w": n(ks[9], (L, POOL_GROUPS, POOL_GROUP_DIM, POOL_GROUP_DIM), POOL_GROUP_DIM ** -0.5),
        "pool_scale": 1.0 + n(ks[10], (L, POOL_WIDTH), 0.02),
        "w_pool_proj": n(ks[11], (L, POOL_WIDTH, D), POOL_WIDTH ** -0.5 * DN_BETA),
        "w_out": n(ks[12], (L, D, D), D ** -0.5 * DN_BETA),
        "ln1_g": 1.0 + n(ks[13], (L, D), 0.02),
        "ln1_b": n(ks[14], (L, D), 0.01),
        "peer_wq": n(ks[15], (L, D, PEER_HEADS * PEER_QUERY_DIM), D ** -0.5),
        "peer_subkeys": n(ks[16], (L, PEER_HEADS, 2, PEER_KEYS, PEER_HALF), PEER_HALF ** -0.5),
        "peer_u": n(ks[17], (L, PEER_EXPERTS, D), D ** -0.5),
        "peer_v": n(ks[18], (L, PEER_EXPERTS, D), DN_BETA),
        "ln2_g": 1.0 + n(ks[19], (L, D), 0.02),
        "ln2_b": n(ks[20], (L, D), 0.01),
    }


def reference(x, w_in, b_gate, lambda_q1, lambda_k1, lambda_q2, lambda_k2, subln_g, w_attn_proj,
              pool_w, pool_scale, w_pool_proj, w_out, ln1_g, ln1_b, peer_wq, peer_subkeys,
              peer_u, peer_v, ln2_g, ln2_b):
    h = x
    for l in range(DEPTH):
        lam_init = 0.8 - 0.6 * math.exp(-0.3 * l)
        mix = hybrid_mixer(h, w_in[l], b_gate[l], lambda_q1[l], lambda_k1[l], lambda_q2[l], lambda_k2[l],
                           subln_g[l], w_attn_proj[l], pool_w[l], pool_scale[l], w_pool_proj[l],
                           w_out[l], lam_init)
        h = layer_norm(DN_ALPHA * h + mix, ln1_g[l], ln1_b[l])
        ffn = peer_ffn(h, peer_wq[l], peer_subkeys[l], peer_u[l], peer_v[l])
        h = layer_norm(DN_ALPHA * h + ffn, ln2_g[l], ln2_b[l])
    return h
```

```python
import functools
import math

import jax
import jax.numpy as jnp
import numpy as np
from jax import lax
from jax.experimental import pallas as pl
from jax.experimental.pallas import tpu as pltpu

F32 = jnp.float32
BF16 = jnp.bfloat16

N_HEADS = 8
HEAD_DIM = 64
V_HEAD_DIM = 2 * HEAD_DIM
POOL_WINDOWS = (2, 4, 8, 16)
POOL_GROUP_DIM = 128
POOL_HALO = 16
PEER_HEADS = 8
PEER_KEYS = 128
PEER_HALF = 128
PEER_TOPK = 16
PEER_SLOTS = PEER_HEADS * PEER_TOPK
LN_EPS = 1e-5

LANES = 128
VMEM_LIMIT_BYTES = 56 * 1024 * 1024
PROJ_ROWS = 512
ATTN_BLOCK = 256
ROUTE_ROWS = 256
PEER_ROWS = 128
CHUNK_STRIDE = PEER_SLOTS + 1


def _cparams(sem):
    return pltpu.CompilerParams(dimension_semantics=sem, vmem_limit_bytes=VMEM_LIMIT_BYTES)


def _full(shape):
    n = len(shape)
    return pl.BlockSpec(shape, lambda *_: (0,) * n)


def _proj_kernel(x_ref, wqT_ref, wk_ref, wvT_ref, wp_ref, wg_ref, bg_ref,
                 qT_ref, k_ref, vT_ref, p_ref, gate_ref, *, scale, n_vblk, ta):
    xb = x_ref[...].astype(BF16)
    nt = (((1,), (1,)), ((), ()))
    qT = lax.dot_general(wqT_ref[...], xb, nt, preferred_element_type=F32)
    qT_ref[...] = (qT * scale).astype(BF16)
    k_ref[...] = jnp.dot(xb, wk_ref[...], preferred_element_type=F32).astype(BF16)
    vT = lax.dot_general(wvT_ref[...], xb, nt, preferred_element_type=F32).astype(BF16)
    for n in range(n_vblk):
        vT_ref[n] = vT[:, n * ta:(n + 1) * ta]
    p_ref[...] = jnp.dot(xb, wp_ref[...], preferred_element_type=F32)
    g = jnp.dot(xb, wg_ref[...], preferred_element_type=F32) + bg_ref[...]
    gate_ref[...] = jax.nn.sigmoid(g).astype(BF16)


def _project(x2, wqT, wk, wvT, wp, wg, bg, *, tm, ta, scale):
    T, D = x2.shape
    wqk, wv, wpool, wgate = wqT.shape[0], wvT.shape[0], wp.shape[1], wg.shape[1]
    n_vblk = tm // ta
    kern = functools.partial(_proj_kernel, scale=scale, n_vblk=n_vblk, ta=ta)
    return pl.pallas_call(
        kern,
        grid=(T // tm,),
        in_specs=[
            pl.BlockSpec((tm, D), lambda i: (i, 0)),
            _full(wqT.shape), _full(wk.shape), _full(wvT.shape), _full(wp.shape), _full(wg.shape),
            _full(bg.shape),
        ],
        out_specs=[
            pl.BlockSpec((wqk, tm), lambda i: (0, i)),
            pl.BlockSpec((tm, wqk), lambda i: (i, 0)),
            pl.BlockSpec((n_vblk, wv, ta), lambda i: (i, 0, 0)),
            pl.BlockSpec((tm, wpool), lambda i: (i, 0)),
            pl.BlockSpec((tm, wgate), lambda i: (i, 0)),
        ],
        out_shape=[
            jax.ShapeDtypeStruct((wqk, T), BF16),
            jax.ShapeDtypeStruct((T, wqk), BF16),
            jax.ShapeDtypeStruct((T // ta, wv, ta), BF16),
            jax.ShapeDtypeStruct((T, wpool), F32),
            jax.ShapeDtypeStruct((T, wgate), BF16),
        ],
        compiler_params=_cparams(("arbitrary",)),
        name="input_projection",
    )(x2, wqT, wk, wvT, wp, wg, bg)


def _attn_kernel(slopes_ref, lamp_ref, g_ref, qT_ref, k_ref, vT_ref, o_ref,
                 m_ref, l_ref, acc_ref, *, ta, lam_init):
    h = pl.program_id(1)
    i = pl.program_id(2)
    slope = slopes_ref[h]
    d = HEAD_DIM
    qT = qT_ref[...]
    z = jnp.zeros((d, ta), BF16)
    qq = jnp.concatenate([jnp.concatenate([qT[:d], z], axis=0),
                          jnp.concatenate([z, qT[d:]], axis=0)], axis=1)
    m_ref[...] = jnp.full(m_ref.shape, -jnp.inf, F32)
    l_ref[...] = jnp.zeros(l_ref.shape, F32)
    acc_ref[...] = jnp.zeros(acc_ref.shape, F32)
    rows = lax.broadcasted_iota(jnp.int32, (ta, LANES), 0)

    def step(j, masked):
        kb = k_ref[pl.ds(pl.multiple_of(j * ta, ta), ta), :]
        s = jnp.dot(kb, qq, preferred_element_type=F32)
        kbias = (rows + (j - i) * ta).astype(F32) * slope
        s = s + jnp.concatenate([kbias] * (2 * ta // LANES), axis=1)
        if masked:
            key = lax.broadcasted_iota(jnp.int32, (ta, 2 * ta), 0)
            qry = lax.broadcasted_iota(jnp.int32, (ta, 2 * ta), 1)
            qry = jnp.where(qry >= ta, qry - ta, qry)
            s = jnp.where(key > qry, -jnp.inf, s)
        m_old = m_ref[...]
        m_new = jnp.maximum(m_old, jnp.max(s, axis=0, keepdims=True))
        alpha = jnp.exp(m_old - m_new)
        pr = jnp.exp(s - m_new)
        l_ref[...] = alpha * l_ref[...] + jnp.sum(pr, axis=0, keepdims=True)
        acc_ref[...] = alpha * acc_ref[...] + jnp.dot(vT_ref[j], pr.astype(BF16),
                                                      preferred_element_type=F32)
        m_ref[...] = m_new

    def body(j, carry):
        step(j, False)
        return carry

    lax.fori_loop(0, i, body, 0)
    step(i, True)

    lamp = lamp_ref[...]
    lam = (jnp.exp(jnp.sum(lamp[0:1] * lamp[1:2], axis=1, keepdims=True))
           - jnp.exp(jnp.sum(lamp[2:3] * lamp[3:4], axis=1, keepdims=True)) + lam_init)
    l = l_ref[...]
    acc = acc_ref[...]
    oT = acc[:, :ta] / l[:, :ta] - lam * (acc[:, ta:] / l[:, ta:])
    ms = jnp.mean(oT * oT, axis=0, keepdims=True)
    oT = oT * lax.rsqrt(ms + LN_EPS)
    o = oT.T * g_ref[...] * (1.0 - lam_init)
    o_ref[...] = o.astype(BF16)


def _attention(slopes, lamp, subln_g, qT, k, vTb, *, B, S, ta, lam_init):
    T = B * S
    nq = S // ta
    vd = V_HEAD_DIM
    kern = functools.partial(_attn_kernel, ta=ta, lam_init=lam_init)
    return pl.pallas_call(
        kern,
        grid=(B, N_HEADS, nq),
        in_specs=[
            pl.BlockSpec(memory_space=pltpu.SMEM),
            _full(lamp.shape),
            _full(subln_g.shape),
            pl.BlockSpec((vd, ta), lambda b, h, i: (h, b * nq + i)),
            pl.BlockSpec((S, vd), lambda b, h, i: (b, h)),
            pl.BlockSpec((nq, vd, ta), lambda b, h, i: (b, h, 0)),
        ],
        out_specs=pl.BlockSpec((ta, vd), lambda b, h, i: (b * nq + i, h)),
        out_shape=jax.ShapeDtypeStruct((T, N_HEADS * vd), BF16),
        scratch_shapes=[
            pltpu.VMEM((1, 2 * ta), F32),
            pltpu.VMEM((1, 2 * ta), F32),
            pltpu.VMEM((vd, 2 * ta), F32),
        ],
        compiler_params=_cparams(("arbitrary", "arbitrary", "arbitrary")),
        name="diff_attention",
    )(slopes, lamp, subln_g, qT, k, vTb)


def _layer_norm_rows(r, g, b):
    mu = jnp.mean(r, axis=-1, keepdims=True)
    dlt = r - mu
    var = jnp.mean(dlt * dlt, axis=-1, keepdims=True)
    return dlt * lax.rsqrt(var + LN_EPS) * g + b


def _merge_kernel(o_ref, p_ref, ph_ref, gate_ref, x_ref, wap_ref, pw_ref, ps_ref, wpp_ref,
                  wout_ref, g1_ref, b1_ref, h1_ref, ext_ref, *, tm, seq, dn_alpha):
    i = pl.program_id(0)
    t0 = (i * tm) % seq
    hl = POOL_HALO
    ext_ref[0:hl, :] = jnp.where(t0 == 0, 0.0, ph_ref[...])
    ext_ref[hl:hl + tm, :] = p_ref[...]
    pos = (t0 + lax.broadcasted_iota(jnp.int32, (tm, POOL_GROUP_DIM), 0)).astype(F32)
    pm = []
    for gi, w in enumerate(POOL_WINDOWS):
        sl = slice(gi * POOL_GROUP_DIM, (gi + 1) * POOL_GROUP_DIM)
        cur = ext_ref[hl:hl + tm, sl]
        win = cur
        for back in range(1, w):
            win = win + ext_ref[hl - back:hl - back + tm, sl]
        cnt = jnp.minimum(float(w), pos + 1.0)
        pooled = win / cnt - cur
        pm.append(jnp.dot(pooled.astype(BF16), pw_ref[gi], preferred_element_type=F32))
    pm = jnp.concatenate(pm, axis=1) * ps_ref[...]
    pool_branch = jnp.dot(pm.astype(BF16), wpp_ref[...], preferred_element_type=F32)
    attn_branch = jnp.dot(o_ref[...], wap_ref[...], preferred_element_type=F32)
    dm = attn_branch.shape[1]
    gates = gate_ref[...].astype(F32)
    merged = gates[:, :dm] * attn_branch + gates[:, dm:] * pool_branch
    mix = jnp.dot(merged.astype(BF16), wout_ref[...], preferred_element_type=F32)
    h1_ref[...] = _layer_norm_rows(dn_alpha * x_ref[...] + mix, g1_ref[...], b1_ref[...])


def _merge(o, p, gates, x2, wap, pw, ps, wpp, wout, g1, b1, *, tm, seq, dn_alpha):
    T, D = x2.shape
    pwid = p.shape[1]
    hb = tm // POOL_HALO
    kern = functools.partial(_merge_kernel, tm=tm, seq=seq, dn_alpha=dn_alpha)
    return pl.pallas_call(
        kern,
        grid=(T // tm,),
        in_specs=[
            pl.BlockSpec((tm, o.shape[1]), lambda i: (i, 0)),
            pl.BlockSpec((tm, pwid), lambda i: (i, 0)),
            pl.BlockSpec((POOL_HALO, pwid), lambda i: (jnp.maximum(i * hb - 1, 0), 0)),
            pl.BlockSpec((tm, gates.shape[1]), lambda i: (i, 0)),
            pl.BlockSpec((tm, D), lambda i: (i, 0)),
            _full(wap.shape), _full(pw.shape), _full(ps.shape), _full(wpp.shape), _full(wout.shape),
            _full(g1.shape), _full(b1.shape),
        ],
        out_specs=pl.BlockSpec((tm, D), lambda i: (i, 0)),
        out_shape=jax.ShapeDtypeStruct((T, D), F32),
        scratch_shapes=[pltpu.VMEM((POOL_HALO + tm, pwid), F32)],
        compiler_params=_cparams(("arbitrary",)),
        name="merge_ln1",
    )(o, p, p, gates, x2, wap, pw, ps, wpp, wout, g1, b1)


def _top_rows(s, k, payload=None):
    n = s.shape[0]
    row = lax.broadcasted_iota(jnp.int32, s.shape, 0)
    vals, picks = [], []
    for _ in range(k):
        m = jnp.max(s, axis=0, keepdims=True)
        am = jnp.min(jnp.where(s == m, row, n), axis=0, keepdims=True)
        hit = row == am
        vals.append(m)
        if payload is None:
            picks.append(am)
        else:
            picks.append(jnp.sum(jnp.where(hit, payload, 0), axis=0, keepdims=True))
        s = jnp.where(hit, -jnp.inf, s)
    return jnp.concatenate(vals, axis=0), jnp.concatenate(picks, axis=0)


def _route_kernel(y_ref, wq_ref, sk_ref, idx_ref, gate_ref):
    yb = y_ref[...].astype(BF16)
    nt = (((1,), (1,)), ((), ()))
    kk = PEER_TOPK

    def head(h, carry):
        q = jnp.dot(yb, wq_ref[h], preferred_element_type=F32).astype(BF16)
        top = []
        for part in range(2):
            qp = q[:, part * PEER_HALF:(part + 1) * PEER_HALF]
            sT = lax.dot_general(sk_ref[2 * h + part], qp, nt, preferred_element_type=F32)
            top.append(_top_rows(sT, kk))
        (s1, i1), (s2, i2) = top
        cand = jnp.concatenate([s1[a:a + 1] + s2 for a in range(kk)], axis=0)
        eid = jnp.concatenate([i1[a:a + 1] * PEER_KEYS + i2 for a in range(kk)], axis=0)
        sc, ids = _top_rows(cand, kk, payload=eid)
        e = jnp.exp(sc - sc[0:1])
        gate = e / jnp.sum(e, axis=0, keepdims=True)
        r0 = pl.multiple_of(h * kk, kk)
        idx_ref[pl.ds(r0, kk), :] = ids
        gate_ref[pl.ds(r0, kk), :] = gate
        return carry

    lax.fori_loop(0, PEER_HEADS, head, 0)


def _route(h1, wq3, sk, *, tm):
    T, D = h1.shape
    return pl.pallas_call(
        _route_kernel,
        grid=(T // tm,),
        in_specs=[pl.BlockSpec((tm, D), lambda i: (i, 0)), _full(wq3.shape), _full(sk.shape)],
        out_specs=[pl.BlockSpec((PEER_SLOTS, tm), lambda i: (0, i)),
                   pl.BlockSpec((PEER_SLOTS, tm), lambda i: (0, i))],
        out_shape=[jax.ShapeDtypeStruct((PEER_SLOTS, T), jnp.int32),
                   jax.ShapeDtypeStruct((PEER_SLOTS, T), F32)],
        compiler_params=_cparams(("arbitrary",)),
        name="peer_route",
    )(h1, wq3, sk)


def _pack_table(t):
    n, dd = t.shape
    tb = t.astype(BF16)
    lo = lax.bitcast_convert_type(tb[:, :dd // 2], jnp.uint16).astype(jnp.uint32)
    hi = lax.bitcast_convert_type(tb[:, dd // 2:], jnp.uint16).astype(jnp.uint32)
    return (lo | (hi << 16)).reshape(n, dd // (2 * LANES), LANES)


def _unpack_row(w):
    lo = lax.bitcast_convert_type(w << 16, F32)
    hi = lax.bitcast_convert_type(w & jnp.uint32(0xFFFF0000), F32)
    return lo, hi


def _load_table_once(tab_hbm, tab_vmem, sem):
    @pl.when(pl.program_id(0) == 0)
    def _():
        cp = pltpu.make_async_copy(tab_hbm, tab_vmem, sem)
        cp.start()
        cp.wait()


def _score_kernel(idx_ref, y_ref, gate_ref, tab_hbm, w_ref, tab_vmem, sem, pa_ref, pb_ref, *, tb):
    _load_table_once(tab_hbm, tab_vmem, sem)
    nsl = PEER_SLOTS
    rs = tab_vmem.shape[1]
    st = CHUNK_STRIDE
    pb_ref[...] = jnp.zeros(pb_ref.shape, F32)
    lane = lax.broadcasted_iota(jnp.int32, (nsl, tb), 1)

    def gather(c, dst):
        ylo = y_ref[c, 0]
        yhi = y_ref[c, 1]
        for e in range(nsl):
            lo, hi = _unpack_row(tab_vmem[idx_ref[c, e]])
            dst[pl.ds(e, rs, stride=st), :] = lo * ylo + hi * yhi

    def fold(c, src, act):
        tot = src[0:nsl, :]
        for s in range(1, rs):
            tot = tot + src[s * st:s * st + nsl, :]
        a = jnp.sum(tot, axis=1, keepdims=True)
        return jnp.where(lane == c, a, act)

    def pair(kp, act):
        c0 = 2 * kp
        gather(c0, pa_ref)
        act = fold(c0 - 1, pb_ref, act)
        gather(c0 + 1, pb_ref)
        act = fold(c0, pa_ref, act)
        return act

    act = lax.fori_loop(0, tb // 2, pair, jnp.zeros((nsl, tb), F32))
    act = fold(tb - 1, pb_ref, act)
    gelu = 0.5 * act * (1.0 + lax.erf(act * (2.0 ** -0.5)))
    w_ref[...] = gate_ref[...] * gelu


def _expert_scores(idx, y4, gateT, utab, *, tb):
    T = idx.shape[0]
    kern = functools.partial(_score_kernel, tb=tb)
    rs = utab.shape[1]
    prow = CHUNK_STRIDE * rs
    return pl.pallas_call(
        kern,
        grid=(T // tb,),
        in_specs=[
            pl.BlockSpec((tb, PEER_SLOTS), lambda i: (i, 0), memory_space=pltpu.SMEM),
            pl.BlockSpec((tb,) + y4.shape[1:], lambda i: (i, 0, 0, 0)),
            pl.BlockSpec((PEER_SLOTS, tb), lambda i: (0, i)),
            pl.BlockSpec(memory_space=pl.ANY),
        ],
        out_specs=pl.BlockSpec((PEER_SLOTS, tb), lambda i: (0, i)),
        out_shape=jax.ShapeDtypeStruct((PEER_SLOTS, T), F32),
        scratch_shapes=[
            pltpu.VMEM(utab.shape, jnp.uint32),
            pltpu.SemaphoreType.DMA(()),
            pltpu.VMEM((prow, LANES), F32),
            pltpu.VMEM((prow, LANES), F32),
        ],
        compiler_params=_cparams(("arbitrary",)),
        name="peer_scores",
    )(idx, y4, gateT, utab)


def _mix_kernel(idx_ref, w_ref, tab_hbm, f_ref, tab_vmem, sem, *, tb):
    _load_table_once(tab_hbm, tab_vmem, sem)
    nsl = PEER_SLOTS
    rs = tab_vmem.shape[1]
    nacc = 4

    def tok(c, carry):
        alo = [jnp.zeros((rs, LANES), F32) for _ in range(nacc)]
        ahi = [jnp.zeros((rs, LANES), F32) for _ in range(nacc)]
        for e in range(nsl):
            lo, hi = _unpack_row(tab_vmem[idx_ref[c, e]])
            wv = jnp.full((rs, LANES), w_ref[c, e], F32)
            alo[e % nacc] = alo[e % nacc] + wv * lo
            ahi[e % nacc] = ahi[e % nacc] + wv * hi
        f_ref[c, 0] = (alo[0] + alo[1]) + (alo[2] + alo[3])
        f_ref[c, 1] = (ahi[0] + ahi[1]) + (ahi[2] + ahi[3])
        return carry

    lax.fori_loop(0, tb, tok, 0)


def _expert_mix(idx, w, vtab, *, tb):
    T = idx.shape[0]
    rs = vtab.shape[1]
    kern = functools.partial(_mix_kernel, tb=tb)
    return pl.pallas_call(
        kern,
        grid=(T // tb,),
        in_specs=[
            pl.BlockSpec((tb, PEER_SLOTS), lambda i: (i, 0), memory_space=pltpu.SMEM),
            pl.BlockSpec((tb, PEER_SLOTS), lambda i: (i, 0), memory_space=pltpu.SMEM),
            pl.BlockSpec(memory_space=pl.ANY),
        ],
        out_specs=pl.BlockSpec((tb, 2, rs, LANES), lambda i: (i, 0, 0, 0)),
        out_shape=jax.ShapeDtypeStruct((T, 2, rs, LANES), F32),
        scratch_shapes=[pltpu.VMEM(vtab.shape, jnp.uint32), pltpu.SemaphoreType.DMA(())],
        compiler_params=_cparams(("arbitrary",)),
        name="peer_mix",
    )(idx, w, vtab)


def _ln2_kernel(h_ref, f_ref, g_ref, b_ref, o_ref, *, dn_alpha):
    o_ref[...] = _layer_norm_rows(dn_alpha * h_ref[...] + f_ref[...], g_ref[...], b_ref[...])


def _residual_ln(h1, ffn, g, b, *, tm, dn_alpha):
    T, D = h1.shape
    return pl.pallas_call(
        functools.partial(_ln2_kernel, dn_alpha=dn_alpha),
        grid=(T // tm,),
        in_specs=[pl.BlockSpec((tm, D), lambda i: (i, 0)), pl.BlockSpec((tm, D), lambda i: (i, 0)),
                  _full(g.shape), _full(b.shape)],
        out_specs=pl.BlockSpec((tm, D), lambda i: (i, 0)),
        out_shape=jax.ShapeDtypeStruct((T, D), F32),
        compiler_params=_cparams(("arbitrary",)),
        name="residual_ln2",
    )(h1, ffn, g, b)


def _layer(h2d, B, S, depth, lam_init, w_in, b_gate, lq1, lk1, lq2, lk2, subln_g, w_attn_proj, pool_w,
           pool_scale, w_pool_proj, w_out, ln1_g, ln1_b, peer_wq, peer_subkeys, peer_u, peer_v,
           ln2_g, ln2_b):
    T, D = h2d.shape
    dn_alpha = (2.0 * depth) ** 0.25
    wqk = 2 * N_HEADS * HEAD_DIM
    wv = N_HEADS * V_HEAD_DIM
    wpool = len(POOL_WINDOWS) * POOL_GROUP_DIM
    tm = min(PROJ_ROWS, S)
    ta = min(ATTN_BLOCK, S)
    c0, c1, c2, c3 = wqk, 2 * wqk, 2 * wqk + wv, 2 * wqk + wv + wpool
    wb = w_in.astype(BF16)
    row = lambda a: a.reshape(1, -1).astype(F32)

    qT, k, vTb, p, gates = _project(
        h2d, wb[:, :c0].T, wb[:, c0:c1], wb[:, c1:c2].T, wb[:, c2:c3], wb[:, c3:], row(b_gate),
        tm=tm, ta=ta, scale=HEAD_DIM ** -0.5)

    slopes = jnp.asarray(np.array([2.0 ** (-8.0 * (h + 1) / N_HEADS) for h in range(N_HEADS)],
                                  dtype=np.float32))
    lamp = jnp.stack([lq1, lk1, lq2, lk2]).astype(F32)
    o = _attention(slopes, lamp, row(subln_g), qT, k, vTb, B=B, S=S, ta=ta, lam_init=lam_init)

    h1 = _merge(o, p, gates, h2d, w_attn_proj.astype(BF16), pool_w.astype(BF16), row(pool_scale),
                w_pool_proj.astype(BF16), w_out.astype(BF16), row(ln1_g), row(ln1_b),
                tm=tm, seq=S, dn_alpha=dn_alpha)

    wq3 = peer_wq.astype(BF16).reshape(D, PEER_HEADS, 2 * PEER_HALF).transpose(1, 0, 2)
    sk = peer_subkeys.astype(BF16).reshape(PEER_HEADS * 2, PEER_KEYS, PEER_HALF)
    idxT, gateT = _route(h1, wq3, sk, tm=min(ROUTE_ROWS, T))

    tb = min(PEER_ROWS, T)
    rs = D // (2 * LANES)
    idx = idxT.T
    wT = _expert_scores(idx, h1.reshape(T, 2, rs, LANES), gateT, _pack_table(peer_u), tb=tb)
    ffn = _expert_mix(idx, wT.T, _pack_table(peer_v), tb=tb)
    return _residual_ln(h1, ffn.reshape(T, D), row(ln2_g), row(ln2_b), tm=tm, dn_alpha=dn_alpha)


def kernel(x, w_in, b_gate, lambda_q1, lambda_k1, lambda_q2, lambda_k2, subln_g, w_attn_proj, pool_w,
           pool_scale, w_pool_proj, w_out, ln1_g, ln1_b, peer_wq, peer_subkeys, peer_u, peer_v,
           ln2_g, ln2_b):
    B, S, D = x.shape
    depth = w_in.shape[0]
    h = x.reshape(B * S, D)
    for l in range(depth):
        lam_init = 0.8 - 0.6 * math.exp(-0.3 * l)
        h = _layer(h, B, S, depth, lam_init, w_in[l], b_gate[l], lambda_q1[l], lambda_k1[l],
                   lambda_q2[l], lambda_k2[l], subln_g[l], w_attn_proj[l], pool_w[l], pool_scale[l],
                   w_pool_proj[l], w_out[l], ln1_g[l], ln1_b[l], peer_wq[l], peer_subkeys[l],
                   peer_u[l], peer_v[l], ln2_g[l], ln2_b[l])
    return h.reshape(B, S, D)
```

```python
import functools
import math

import jax
import jax.numpy as jnp
import numpy as np
from jax import lax
from jax.experimental import pallas as pl
from jax.experimental.pallas import tpu as pltpu

F32 = jnp.float32
BF16 = jnp.bfloat16

N_HEADS = 8
HEAD_DIM = 64
V_HEAD_DIM = 2 * HEAD_DIM
POOL_WINDOWS = (2, 4, 8, 16)
POOL_GROUP_DIM = 128
POOL_HALO = 16
PEER_HEADS = 8
PEER_KEYS = 128
PEER_HALF = 128
PEER_TOPK = 16
PEER_SLOTS = PEER_HEADS * PEER_TOPK
LN_EPS = 1e-5
LOG2E = math.log2(math.e)

LANES = 128
SUBLANES = 8
VMEM_LIMIT_BYTES = 56 * 1024 * 1024
PROJ_ROWS = 512
ATTN_BLOCK = 256
ROUTE_ROWS = 256
PEER_ROWS = 128
CHUNK_STRIDE = PEER_SLOTS + 1


def _cparams(sem):
    return pltpu.CompilerParams(dimension_semantics=sem, vmem_limit_bytes=VMEM_LIMIT_BYTES)


def _full(shape):
    n = len(shape)
    return pl.BlockSpec(shape, lambda *_: (0,) * n)


def _proj_kernel(x_ref, wqT_ref, wk_ref, wvT_ref, wp_ref, wg_ref, bg_ref,
                 qT_ref, k_ref, vT_ref, p_ref, gate_ref, *, scale, n_vblk, ta):
    xb = x_ref[...].astype(BF16)
    nt = (((1,), (1,)), ((), ()))
    qT = lax.dot_general(wqT_ref[...], xb, nt, preferred_element_type=F32)
    qT_ref[...] = (qT * scale).astype(BF16)
    k_ref[...] = jnp.dot(xb, wk_ref[...], preferred_element_type=F32).astype(BF16)
    vT = lax.dot_general(wvT_ref[...], xb, nt, preferred_element_type=F32).astype(BF16)
    for n in range(n_vblk):
        vT_ref[n] = vT[:, n * ta:(n + 1) * ta]
    p_ref[...] = jnp.dot(xb, wp_ref[...], preferred_element_type=F32)
    g = jnp.dot(xb, wg_ref[...], preferred_element_type=F32) + bg_ref[...]
    gate_ref[...] = jax.nn.sigmoid(g).astype(BF16)


def _project(x2, wqT, wk, wvT, wp, wg, bg, *, tm, ta, scale):
    T, D = x2.shape
    wqk, wv, wpool, wgate = wqT.shape[0], wvT.shape[0], wp.shape[1], wg.shape[1]
    n_vblk = tm // ta
    kern = functools.partial(_proj_kernel, scale=scale, n_vblk=n_vblk, ta=ta)
    return pl.pallas_call(
        kern,
        grid=(T // tm,),
        in_specs=[
            pl.BlockSpec((tm, D), lambda i: (i, 0)),
            _full(wqT.shape), _full(wk.shape), _full(wvT.shape), _full(wp.shape), _full(wg.shape),
            _full(bg.shape),
        ],
        out_specs=[
            pl.BlockSpec((wqk, tm), lambda i: (0, i)),
            pl.BlockSpec((tm, wqk), lambda i: (i, 0)),
            pl.BlockSpec((n_vblk, wv, ta), lambda i: (i, 0, 0)),
            pl.BlockSpec((tm, wpool), lambda i: (i, 0)),
            pl.BlockSpec((tm, wgate), lambda i: (i, 0)),
        ],
        out_shape=[
            jax.ShapeDtypeStruct((wqk, T), BF16),
            jax.ShapeDtypeStruct((T, wqk), BF16),
            jax.ShapeDtypeStruct((T // ta, wv, ta), BF16),
            jax.ShapeDtypeStruct((T, wpool), F32),
            jax.ShapeDtypeStruct((T, wgate), BF16),
        ],
        compiler_params=_cparams(("arbitrary",)),
        name="input_projection",
    )(x2, wqT, wk, wvT, wp, wg, bg)


def _attn_kernel(slopes_ref, lamp_ref, g_ref, qT_ref, k_ref, vT_ref, o_ref,
                 m_ref, l_ref, acc_ref, al_ref, base_ref, s0_ref, s1_ref, p0_ref, p1_ref,
                 *, ta, lam_init):
    h = pl.program_id(1)
    i = pl.program_id(2)
    slope = slopes_ref[h]
    d = HEAD_DIM
    nrep = 2 * ta // LANES
    qT = qT_ref[...]
    z = jnp.zeros((d, ta), BF16)
    qq = jnp.concatenate([jnp.concatenate([qT[:d], z], axis=0),
                          jnp.concatenate([z, qT[d:]], axis=0)], axis=1)
    m_ref[...] = jnp.full(m_ref.shape, -jnp.inf, F32)
    l_ref[...] = jnp.zeros(l_ref.shape, F32)
    acc_ref[...] = jnp.zeros(acc_ref.shape, F32)
    al_ref[...] = jnp.ones(al_ref.shape, F32)
    p1_ref[...] = jnp.zeros(p1_ref.shape, BF16)
    base_ref[...] = lax.broadcasted_iota(jnp.int32, (ta, LANES), 0).astype(F32) * slope

    def scores(j, s_ref):
        kb = k_ref[pl.ds(pl.multiple_of(j * ta, ta), ta), :]
        s_ref[...] = jnp.dot(kb, qq, preferred_element_type=F32)

    def softmax(j, s_ref, p_ref, masked):
        off = ((j - i) * ta).astype(F32) * slope
        t = s_ref[...] + jnp.concatenate([base_ref[...]] * nrep, axis=1)
        if masked:
            key = j * ta + lax.broadcasted_iota(jnp.int32, (ta, 2 * ta), 0)
            col = lax.broadcasted_iota(jnp.int32, (ta, 2 * ta), 1)
            qry = i * ta + jnp.where(col >= ta, col - ta, col)
            t = jnp.where(key > qry, -jnp.inf, t)
        m_old = m_ref[...]
        m_new = jnp.maximum(m_old, jnp.max(t, axis=0, keepdims=True) + off)
        alpha = jnp.exp2(m_old - m_new)
        pr = jnp.exp2(t + (off - m_new))
        l_ref[...] = alpha * l_ref[...] + jnp.sum(pr, axis=0, keepdims=True)
        p_ref[...] = pr.astype(BF16)
        m_ref[...] = m_new
        return alpha

    def values(j, p_ref):
        vb = vT_ref[jnp.maximum(j, 0)]
        acc_ref[...] = al_ref[...] * acc_ref[...] + jnp.dot(vb, p_ref[...],
                                                             preferred_element_type=F32)

    def half(j, s_cur, s_nxt, p_cur, p_prv, masked, prefetch):
        if prefetch:
            scores(j + 1, s_nxt)
        alpha = softmax(j, s_cur, p_cur, masked)
        values(j - 1, p_prv)
        al_ref[...] = alpha

    def pair(t, masked, last):
        j0 = 2 * t
        half(j0, s0_ref, s1_ref, p0_ref, p1_ref, masked, True)
        half(j0 + 1, s1_ref, s0_ref, p1_ref, p0_ref, masked, not last)

    scores(0, s0_ref)
    npairs = i // 2

    def body(t, carry):
        pair(t, False, False)
        return carry

    lax.fori_loop(0, npairs, body, 0)
    pair(npairs, True, True)
    values(2 * npairs + 1, p1_ref)

    lamp = lamp_ref[...]
    lam = (jnp.exp(jnp.sum(lamp[0:1] * lamp[1:2], axis=1, keepdims=True))
           - jnp.exp(jnp.sum(lamp[2:3] * lamp[3:4], axis=1, keepdims=True)) + lam_init)
    l = l_ref[...]
    acc = acc_ref[...]
    oT = acc[:, :ta] / l[:, :ta] - lam * (acc[:, ta:] / l[:, ta:])
    ms = jnp.mean(oT * oT, axis=0, keepdims=True)
    oT = oT * lax.rsqrt(ms + LN_EPS)
    o = oT.T * g_ref[...] * (1.0 - lam_init)
    o_ref[...] = o.astype(BF16)


def _attention(slopes, lamp, subln_g, qT, k, vTb, *, B, S, ta, lam_init):
    T = B * S
    nq = S // ta
    assert nq % 2 == 0, "the paired key-block pipeline needs an even number of blocks per sequence"
    vd = V_HEAD_DIM
    kern = functools.partial(_attn_kernel, ta=ta, lam_init=lam_init)
    return pl.pallas_call(
        kern,
        grid=(B, N_HEADS, nq),
        in_specs=[
            pl.BlockSpec(memory_space=pltpu.SMEM),
            _full(lamp.shape),
            _full(subln_g.shape),
            pl.BlockSpec((vd, ta), lambda b, h, i: (h, b * nq + i)),
            pl.BlockSpec((S, vd), lambda b, h, i: (b, h)),
            pl.BlockSpec((nq, vd, ta), lambda b, h, i: (b, h, 0)),
        ],
        out_specs=pl.BlockSpec((ta, vd), lambda b, h, i: (b * nq + i, h)),
        out_shape=jax.ShapeDtypeStruct((T, N_HEADS * vd), BF16),
        scratch_shapes=[
            pltpu.VMEM((1, 2 * ta), F32),
            pltpu.VMEM((1, 2 * ta), F32),
            pltpu.VMEM((vd, 2 * ta), F32),
            pltpu.VMEM((1, 2 * ta), F32),
            pltpu.VMEM((ta, LANES), F32),
            pltpu.VMEM((ta, 2 * ta), F32),
            pltpu.VMEM((ta, 2 * ta), F32),
            pltpu.VMEM((ta, 2 * ta), BF16),
            pltpu.VMEM((ta, 2 * ta), BF16),
        ],
        compiler_params=_cparams(("arbitrary", "arbitrary", "arbitrary")),
        name="diff_attention",
    )(slopes, lamp, subln_g, qT, k, vTb)


def _layer_norm_rows(r, g, b):
    mu = jnp.mean(r, axis=-1, keepdims=True)
    dlt = r - mu
    var = jnp.mean(dlt * dlt, axis=-1, keepdims=True)
    return dlt * lax.rsqrt(var + LN_EPS) * g + b


def _merge_kernel(o_ref, p_ref, ph_ref, gate_ref, x_ref, wap_ref, pw_ref, ps_ref, wpp_ref,
                  wout_ref, g1_ref, b1_ref, h1_ref, ext_ref, *, tm, seq, dn_alpha):
    i = pl.program_id(0)
    t0 = (i * tm) % seq
    hl = POOL_HALO
    ext_ref[0:hl, :] = jnp.where(t0 == 0, 0.0, ph_ref[...])
    ext_ref[hl:hl + tm, :] = p_ref[...]
    pos = (t0 + lax.broadcasted_iota(jnp.int32, (tm, POOL_GROUP_DIM), 0)).astype(F32)
    pm = []
    for gi, w in enumerate(POOL_WINDOWS):
        sl = slice(gi * POOL_GROUP_DIM, (gi + 1) * POOL_GROUP_DIM)
        cur = ext_ref[hl:hl + tm, sl]
        win = cur
        for back in range(1, w):
            win = win + ext_ref[hl - back:hl - back + tm, sl]
        cnt = jnp.minimum(float(w), pos + 1.0)
        pooled = win / cnt - cur
        pm.append(jnp.dot(pooled.astype(BF16), pw_ref[gi], preferred_element_type=F32))
    pm = jnp.concatenate(pm, axis=1) * ps_ref[...]
    pool_branch = jnp.dot(pm.astype(BF16), wpp_ref[...], preferred_element_type=F32)
    attn_branch = jnp.dot(o_ref[...], wap_ref[...], preferred_element_type=F32)
    dm = attn_branch.shape[1]
    gates = gate_ref[...].astype(F32)
    merged = gates[:, :dm] * attn_branch + gates[:, dm:] * pool_branch
    mix = jnp.dot(merged.astype(BF16), wout_ref[...], preferred_element_type=F32)
    h1_ref[...] = _layer_norm_rows(dn_alpha * x_ref[...] + mix, g1_ref[...], b1_ref[...])


def _merge(o, p, gates, x2, wap, pw, ps, wpp, wout, g1, b1, *, tm, seq, dn_alpha):
    T, D = x2.shape
    pwid = p.shape[1]
    hb = tm // POOL_HALO
    kern = functools.partial(_merge_kernel, tm=tm, seq=seq, dn_alpha=dn_alpha)
    return pl.pallas_call(
        kern,
        grid=(T // tm,),
        in_specs=[
            pl.BlockSpec((tm, o.shape[1]), lambda i: (i, 0)),
            pl.BlockSpec((tm, pwid), lambda i: (i, 0)),
            pl.BlockSpec((POOL_HALO, pwid), lambda i: (jnp.maximum(i * hb - 1, 0), 0)),
            pl.BlockSpec((tm, gates.shape[1]), lambda i: (i, 0)),
            pl.BlockSpec((tm, D), lambda i: (i, 0)),
            _full(wap.shape), _full(pw.shape), _full(ps.shape), _full(wpp.shape), _full(wout.shape),
            _full(g1.shape), _full(b1.shape),
        ],
        out_specs=pl.BlockSpec((tm, D), lambda i: (i, 0)),
        out_shape=jax.ShapeDtypeStruct((T, D), F32),
        scratch_shapes=[pltpu.VMEM((POOL_HALO + tm, pwid), F32)],
        compiler_params=_cparams(("arbitrary",)),
        name="merge_ln1",
    )(o, p, p, gates, x2, wap, pw, ps, wpp, wout, g1, b1)


def _top_rows(s, k, payload=None):
    n = s.shape[0]
    row = lax.broadcasted_iota(jnp.int32, s.shape, 0)
    vals, picks = [], []
    for _ in range(k):
        m = jnp.max(s, axis=0, keepdims=True)
        am = jnp.min(jnp.where(s == m, row, n), axis=0, keepdims=True)
        hit = row == am
        vals.append(m)
        if payload is None:
            picks.append(am)
        else:
            picks.append(jnp.sum(jnp.where(hit, payload, 0), axis=0, keepdims=True))
        s = jnp.where(hit, -jnp.inf, s)
    return jnp.concatenate(vals, axis=0), jnp.concatenate(picks, axis=0)


def _route_kernel(y_ref, wq_ref, sk_ref, idx_ref, gate_ref):
    yb = y_ref[...].astype(BF16)
    tm = yb.shape[0]
    nt = (((1,), (1,)), ((), ()))
    kk = PEER_TOPK
    sub = lax.broadcasted_iota(jnp.int32, (SUBLANES, tm), 0)

    def head(h, carry):
        q = jnp.dot(yb, wq_ref[h], preferred_element_type=F32).astype(BF16)
        top = []
        for part in range(2):
            qp = q[:, part * PEER_HALF:(part + 1) * PEER_HALF]
            sT = lax.dot_general(sk_ref[2 * h + part], qp, nt, preferred_element_type=F32)
            top.append(_top_rows(sT, kk))
        (s1, i1), (s2, i2) = top
        cand, eid = [], []
        for a in range(kk):
            nb = kk // (a + 1)
            if nb > SUBLANES:
                cand.append(s1[a:a + 1] + s2)
                eid.append(i1[a:a + 1] * PEER_KEYS + i2)
            else:
                c8 = s1[a:a + 1] + s2[:SUBLANES]
                cand.append(c8 if nb == SUBLANES else jnp.where(sub < nb, c8, -jnp.inf))
                eid.append(i1[a:a + 1] * PEER_KEYS + i2[:SUBLANES])
        sc, ids = _top_rows(jnp.concatenate(cand, axis=0), kk, payload=jnp.concatenate(eid, axis=0))
        e = jnp.exp(sc - sc[0:1])
        gate = e / jnp.sum(e, axis=0, keepdims=True)
        r0 = pl.multiple_of(h * kk, kk)
        idx_ref[pl.ds(r0, kk), :] = ids
        gate_ref[pl.ds(r0, kk), :] = gate
        return carry

    lax.fori_loop(0, PEER_HEADS, head, 0)


def _route(h1, wq3, sk, *, tm):
    T, D = h1.shape
    return pl.pallas_call(
        _route_kernel,
        grid=(T // tm,),
        in_specs=[pl.BlockSpec((tm, D), lambda i: (i, 0)), _full(wq3.shape), _full(sk.shape)],
        out_specs=[pl.BlockSpec((PEER_SLOTS, tm), lambda i: (0, i)),
                   pl.BlockSpec((PEER_SLOTS, tm), lambda i: (0, i))],
        out_shape=[jax.ShapeDtypeStruct((PEER_SLOTS, T), jnp.int32),
                   jax.ShapeDtypeStruct((PEER_SLOTS, T), F32)],
        compiler_params=_cparams(("arbitrary",)),
        name="peer_route",
    )(h1, wq3, sk)


def _pack_table(t):
    n, dd = t.shape
    tb = t.astype(BF16)
    lo = lax.bitcast_convert_type(tb[:, :dd // 2], jnp.uint16).astype(jnp.uint32)
    hi = lax.bitcast_convert_type(tb[:, dd // 2:], jnp.uint16).astype(jnp.uint32)
    return (lo | (hi << 16)).reshape(n, dd // (2 * LANES), LANES)


def _unpack_words(w):
    lo = lax.bitcast_convert_type(w << 16, F32)
    hi = lax.bitcast_convert_type(w & jnp.uint32(0xFFFF0000), F32)
    return lo, hi


def _gather_rows(idx_ref, tab_ref, c, dst):
    irow = idx_ref.at[c]
    rs = tab_ref.shape[1]
    for e in range(PEER_SLOTS):
        dst[pl.ds(e, rs, stride=CHUNK_STRIDE), :] = tab_ref[irow[e]]


def _token_pipeline(tb, gather, fold, ca_ref, cb_ref):
    gather(0, ca_ref)

    def pair(kp, carry):
        c0 = 2 * kp
        gather(c0 + 1, cb_ref)
        carry = fold(c0, ca_ref, carry)
        gather(jnp.minimum(c0 + 2, tb - 1), ca_ref)
        return fold(c0 + 1, cb_ref, carry)

    return pair


def _score_kernel(idx_ref, y_ref, gate_ref, tab_ref, w_ref, ca_ref, cb_ref, *, tb):
    nsl = PEER_SLOTS
    rs = tab_ref.shape[1]
    st = CHUNK_STRIDE
    half = rs * LANES
    lane = lax.broadcasted_iota(jnp.int32, (nsl, tb), 1)

    def fold(c, src, act):
        tot = None
        yv = y_ref[c]
        for s in range(rs):
            lo, hi = _unpack_words(src[s * st:s * st + nsl, :])
            part = lo * yv[s:s + 1] + hi * yv[rs + s:rs + s + 1]
            tot = part if tot is None else tot + part
        a = jnp.sum(tot, axis=1, keepdims=True)
        return jnp.where(lane == c, a, act)

    gather = functools.partial(_gather_rows, idx_ref, tab_ref)
    pair = _token_pipeline(tb, gather, fold, ca_ref, cb_ref)
    act = lax.fori_loop(0, tb // 2, pair, jnp.zeros((nsl, tb), F32))
    gelu = 0.5 * act * (1.0 + lax.erf(act * (2.0 ** -0.5)))
    w_ref[...] = gate_ref[...] * gelu


def _table_spec(tab):
    return pl.BlockSpec(tab.shape, lambda i: (0, 0, 0), pipeline_mode=pl.Buffered(1))


def _chunk_scratch(tab):
    return pltpu.VMEM((CHUNK_STRIDE * tab.shape[1], LANES), jnp.uint32)


def _expert_scores(idx, y3, gateT, utab, *, tb):
    T = y3.shape[0]
    kern = functools.partial(_score_kernel, tb=tb)
    return pl.pallas_call(
        kern,
        grid=(T // tb,),
        in_specs=[
            pl.BlockSpec((tb, PEER_SLOTS), lambda i: (i, 0), memory_space=pltpu.SMEM),
            pl.BlockSpec((tb,) + y3.shape[1:], lambda i: (i, 0, 0)),
            pl.BlockSpec((PEER_SLOTS, tb), lambda i: (0, i)),
            _table_spec(utab),
        ],
        out_specs=pl.BlockSpec((PEER_SLOTS, tb), lambda i: (0, i)),
        out_shape=jax.ShapeDtypeStruct((PEER_SLOTS, T), F32),
        scratch_shapes=[_chunk_scratch(utab), _chunk_scratch(utab)],
        compiler_params=_cparams(("arbitrary",)),
        name="peer_scores",
    )(idx, y3, gateT, utab)


def _mix_kernel(idx_ref, w_ref, tab_ref, f_ref, ca_ref, cb_ref, *, tb):
    nsl = PEER_SLOTS
    rs = tab_ref.shape[1]
    st = CHUNK_STRIDE
    half = rs * LANES
    lane = lax.broadcasted_iota(jnp.int32, (nsl, tb), 1)

    def fold(c, src, carry):
        wcol = jnp.sum(jnp.where(lane == c, w_ref[...], 0.0), axis=1, keepdims=True)
        wb = jnp.broadcast_to(wcol, (nsl, LANES))
        los, his = [], []
        for s in range(rs):
            lo, hi = _unpack_words(src[s * st:s * st + nsl, :])
            los.append(jnp.sum(lo * wb, axis=0, keepdims=True))
            his.append(jnp.sum(hi * wb, axis=0, keepdims=True))
        f_ref[c] = jnp.concatenate(los + his, axis=0)
        return carry

    gather = functools.partial(_gather_rows, idx_ref, tab_ref)
    pair = _token_pipeline(tb, gather, fold, ca_ref, cb_ref)
    lax.fori_loop(0, tb // 2, pair, 0)


def _expert_mix(idx, wT, vtab, *, tb):
    T = idx.shape[0]
    rows = 2 * vtab.shape[1]
    kern = functools.partial(_mix_kernel, tb=tb)
    return pl.pallas_call(
        kern,
        grid=(T // tb,),
        in_specs=[
            pl.BlockSpec((tb, PEER_SLOTS), lambda i: (i, 0), memory_space=pltpu.SMEM),
            pl.BlockSpec((PEER_SLOTS, tb), lambda i: (0, i)),
            _table_spec(vtab),
        ],
        out_specs=pl.BlockSpec((tb, rows, LANES), lambda i: (i, 0, 0)),
        out_shape=jax.ShapeDtypeStruct((T, rows, LANES), F32),
        scratch_shapes=[_chunk_scratch(vtab), _chunk_scratch(vtab)],
        compiler_params=_cparams(("arbitrary",)),
        name="peer_mix",
    )(idx, wT, vtab)


def _ln2_kernel(h_ref, f_ref, g_ref, b_ref, o_ref, *, dn_alpha):
    o_ref[...] = _layer_norm_rows(dn_alpha * h_ref[...] + f_ref[...], g_ref[...], b_ref[...])


def _residual_ln(h1, ffn, g, b, *, tm, dn_alpha):
    T, D = h1.shape
    return pl.pallas_call(
        functools.partial(_ln2_kernel, dn_alpha=dn_alpha),
        grid=(T // tm,),
        in_specs=[pl.BlockSpec((tm, D), lambda i: (i, 0)), pl.BlockSpec((tm, D), lambda i: (i, 0)),
                  _full(g.shape), _full(b.shape)],
        out_specs=pl.BlockSpec((tm, D), lambda i: (i, 0)),
        out_shape=jax.ShapeDtypeStruct((T, D), F32),
        compiler_params=_cparams(("arbitrary",)),
        name="residual_ln2",
    )(h1, ffn, g, b)


def _layer(h2d, B, S, depth, lam_init, w_in, b_gate, lq1, lk1, lq2, lk2, subln_g, w_attn_proj, pool_w,
           pool_scale, w_pool_proj, w_out, ln1_g, ln1_b, peer_wq, peer_subkeys, peer_u, peer_v,
           ln2_g, ln2_b):
    T, D = h2d.shape
    dn_alpha = (2.0 * depth) ** 0.25
    wqk = 2 * N_HEADS * HEAD_DIM
    wv = N_HEADS * V_HEAD_DIM
    wpool = len(POOL_WINDOWS) * POOL_GROUP_DIM
    tm = min(PROJ_ROWS, S)
    ta = min(ATTN_BLOCK, S)
    c0, c1, c2, c3 = wqk, 2 * wqk, 2 * wqk + wv, 2 * wqk + wv + wpool
    wb = w_in.astype(BF16)
    row = lambda a: a.reshape(1, -1).astype(F32)

    qT, k, vTb, p, gates = _project(
        h2d, wb[:, :c0].T, wb[:, c0:c1], wb[:, c1:c2].T, wb[:, c2:c3], wb[:, c3:], row(b_gate),
        tm=tm, ta=ta, scale=HEAD_DIM ** -0.5 * LOG2E)

    slopes = jnp.asarray(np.array([2.0 ** (-8.0 * (h + 1) / N_HEADS) * LOG2E for h in range(N_HEADS)],
                                  dtype=np.float32))
    lamp = jnp.stack([lq1, lk1, lq2, lk2]).astype(F32)
    o = _attention(slopes, lamp, row(subln_g), qT, k, vTb, B=B, S=S, ta=ta, lam_init=lam_init)

    h1 = _merge(o, p, gates, h2d, w_attn_proj.astype(BF16), pool_w.astype(BF16), row(pool_scale),
                w_pool_proj.astype(BF16), w_out.astype(BF16), row(ln1_g), row(ln1_b),
                tm=tm, seq=S, dn_alpha=dn_alpha)

    wq3 = peer_wq.astype(BF16).reshape(D, PEER_HEADS, 2 * PEER_HALF).transpose(1, 0, 2)
    sk = peer_subkeys.astype(BF16).reshape(PEER_HEADS * 2, PEER_KEYS, PEER_HALF)
    idxT, gateT = _route(h1, wq3, sk, tm=min(ROUTE_ROWS, T))

    tb = min(PEER_ROWS, T)
    idx = idxT.T
    wT = _expert_scores(idx, h1.reshape(T, D // LANES, LANES), gateT, _pack_table(peer_u), tb=tb)
    ffn = _expert_mix(idx, wT, _pack_table(peer_v), tb=tb)
    return _residual_ln(h1, ffn.reshape(T, D), row(ln2_g), row(ln2_b), tm=tm, dn_alpha=dn_alpha)


def kernel(x, w_in, b_gate, lambda_q1, lambda_k1, lambda_q2, lambda_k2, subln_g, w_attn_proj, pool_w,
           pool_scale, w_pool_proj, w_out, ln1_g, ln1_b, peer_wq, peer_subkeys, peer_u, peer_v,
           ln2_g, ln2_b):
    B, S, D = x.shape
    depth = w_in.shape[0]
    h = x.reshape(B * S, D)
    for l in range(depth):
        lam_init = 0.8 - 0.6 * math.exp(-0.3 * l)
        h = _layer(h, B, S, depth, lam_init, w_in[l], b_gate[l], lambda_q1[l], lambda_k1[l],
                   lambda_q2[l], lambda_k2[l], subln_g[l], w_attn_proj[l], pool_w[l], pool_scale[l],
                   w_pool_proj[l], w_out[l], ln1_g[l], ln1_b[l], peer_wq[l], peer_subkeys[l],
                   peer_u[l], peer_v[l], ln2_g[l], ln2_b[l])
    return h.reshape(B, S, D)
```

```python
import functools
import math

import jax
import jax.numpy as jnp
import numpy as np
from jax import lax
from jax.experimental import pallas as pl
from jax.experimental.pallas import tpu as pltpu

F32 = jnp.float32
BF16 = jnp.bfloat16

N_HEADS = 8
HEAD_DIM = 64
V_HEAD_DIM = 2 * HEAD_DIM
POOL_WINDOWS = (2, 4, 8, 16)
POOL_GROUP_DIM = 128
POOL_HALO = 16
PEER_HEADS = 8
PEER_KEYS = 128
PEER_HALF = 128
PEER_TOPK = 16
PEER_SLOTS = PEER_HEADS * PEER_TOPK
LN_EPS = 1e-5
LOG2E = math.log2(math.e)

LANES = 128
SUBLANES = 8
VMEM_LIMIT_BYTES = 56 * 1024 * 1024
PROJ_ROWS = 512
ATTN_BLOCK = 256
ROUTE_ROWS = 256
PEER_ROWS = 128
CHUNK_STRIDE = PEER_SLOTS + 1


def _cparams(sem):
    return pltpu.CompilerParams(dimension_semantics=sem, vmem_limit_bytes=VMEM_LIMIT_BYTES)


def _full(shape):
    n = len(shape)
    return pl.BlockSpec(shape, lambda *_: (0,) * n)


def _proj_kernel(x_ref, wqT_ref, wk_ref, wvT_ref, wp_ref, wg_ref, bg_ref,
                 qT_ref, k_ref, vT_ref, p_ref, gate_ref, *, scale, n_vblk, ta):
    xb = x_ref[...].astype(BF16)
    nt = (((1,), (1,)), ((), ()))
    qT = lax.dot_general(wqT_ref[...], xb, nt, preferred_element_type=F32)
    qT_ref[...] = (qT * scale).astype(BF16)
    k_ref[...] = jnp.dot(xb, wk_ref[...], preferred_element_type=F32).astype(BF16)
    vT = lax.dot_general(wvT_ref[...], xb, nt, preferred_element_type=F32).astype(BF16)
    for n in range(n_vblk):
        vT_ref[n] = vT[:, n * ta:(n + 1) * ta]
    p_ref[...] = jnp.dot(xb, wp_ref[...], preferred_element_type=F32)
    g = jnp.dot(xb, wg_ref[...], preferred_element_type=F32) + bg_ref[...]
    gate_ref[...] = jax.nn.sigmoid(g).astype(BF16)


def _project(x2, wqT, wk, wvT, wp, wg, bg, *, tm, ta, scale):
    T, D = x2.shape
    wqk, wv, wpool, wgate = wqT.shape[0], wvT.shape[0], wp.shape[1], wg.shape[1]
    n_vblk = tm // ta
    kern = functools.partial(_proj_kernel, scale=scale, n_vblk=n_vblk, ta=ta)
    return pl.pallas_call(
        kern,
        grid=(T // tm,),
        in_specs=[
            pl.BlockSpec((tm, D), lambda i: (i, 0)),
            _full(wqT.shape), _full(wk.shape), _full(wvT.shape), _full(wp.shape), _full(wg.shape),
            _full(bg.shape),
        ],
        out_specs=[
            pl.BlockSpec((wqk, tm), lambda i: (0, i)),
            pl.BlockSpec((tm, wqk), lambda i: (i, 0)),
            pl.BlockSpec((n_vblk, wv, ta), lambda i: (i, 0, 0)),
            pl.BlockSpec((tm, wpool), lambda i: (i, 0)),
            pl.BlockSpec((tm, wgate), lambda i: (i, 0)),
        ],
        out_shape=[
            jax.ShapeDtypeStruct((wqk, T), BF16),
            jax.ShapeDtypeStruct((T, wqk), BF16),
            jax.ShapeDtypeStruct((T // ta, wv, ta), BF16),
            jax.ShapeDtypeStruct((T, wpool), F32),
            jax.ShapeDtypeStruct((T, wgate), BF16),
        ],
        compiler_params=_cparams(("arbitrary",)),
        name="input_projection",
    )(x2, wqT, wk, wvT, wp, wg, bg)


def _attn_kernel(slopes_ref, lamp_ref, g_ref, qT_ref, k_ref, vT_ref, o_ref,
                 m_ref, l_ref, acc_ref, al_ref, base_ref, s0_ref, s1_ref, p0_ref, p1_ref,
                 *, ta, lam_init):
    h = pl.program_id(1)
    i = pl.program_id(2)
    slope = slopes_ref[h]
    d = HEAD_DIM
    nrep = 2 * ta // LANES
    qT = qT_ref[...]
    z = jnp.zeros((d, ta), BF16)
    qq = jnp.concatenate([jnp.concatenate([qT[:d], z], axis=0),
                          jnp.concatenate([z, qT[d:]], axis=0)], axis=1)
    m_ref[...] = jnp.full(m_ref.shape, -jnp.inf, F32)
    l_ref[...] = jnp.zeros(l_ref.shape, F32)
    acc_ref[...] = jnp.zeros(acc_ref.shape, F32)
    al_ref[...] = jnp.ones(al_ref.shape, F32)
    p1_ref[...] = jnp.zeros(p1_ref.shape, BF16)
    base_ref[...] = lax.broadcasted_iota(jnp.int32, (ta, LANES), 0).astype(F32) * slope

    def scores(j, s_ref):
        kb = k_ref[pl.ds(pl.multiple_of(j * ta, ta), ta), :]
        s_ref[...] = jnp.dot(kb, qq, preferred_element_type=F32)

    def softmax(j, s_ref, p_ref, masked):
        off = ((j - i) * ta).astype(F32) * slope
        t = s_ref[...] + jnp.concatenate([base_ref[...]] * nrep, axis=1)
        if masked:
            key = j * ta + lax.broadcasted_iota(jnp.int32, (ta, 2 * ta), 0)
            col = lax.broadcasted_iota(jnp.int32, (ta, 2 * ta), 1)
            qry = i * ta + jnp.where(col >= ta, col - ta, col)
            t = jnp.where(key > qry, -jnp.inf, t)
        m_old = m_ref[...]
        m_new = jnp.maximum(m_old, jnp.max(t, axis=0, keepdims=True) + off)
        alpha = jnp.exp2(m_old - m_new)
        pr = jnp.exp2(t + (off - m_new))
        l_ref[...] = alpha * l_ref[...] + jnp.sum(pr, axis=0, keepdims=True)
        p_ref[...] = pr.astype(BF16)
        m_ref[...] = m_new
        return alpha

    def values(j, p_ref):
        vb = vT_ref[jnp.maximum(j, 0)]
        acc_ref[...] = al_ref[...] * acc_ref[...] + jnp.dot(vb, p_ref[...],
                                                             preferred_element_type=F32)

    def half(j, s_cur, s_nxt, p_cur, p_prv, masked, prefetch):
        if prefetch:
            scores(j + 1, s_nxt)
        alpha = softmax(j, s_cur, p_cur, masked)
        values(j - 1, p_prv)
        al_ref[...] = alpha

    def pair(t, masked, last):
        j0 = 2 * t
        half(j0, s0_ref, s1_ref, p0_ref, p1_ref, masked, True)
        half(j0 + 1, s1_ref, s0_ref, p1_ref, p0_ref, masked, not last)

    scores(0, s0_ref)
    npairs = i // 2

    def body(t, carry):
        pair(t, False, False)
        return carry

    lax.fori_loop(0, npairs, body, 0)
    pair(npairs, True, True)
    values(2 * npairs + 1, p1_ref)

    lamp = lamp_ref[...]
    lam = (jnp.exp(jnp.sum(lamp[0:1] * lamp[1:2], axis=1, keepdims=True))
           - jnp.exp(jnp.sum(lamp[2:3] * lamp[3:4], axis=1, keepdims=True)) + lam_init)
    l = l_ref[...]
    acc = acc_ref[...]
    oT = acc[:, :ta] / l[:, :ta] - lam * (acc[:, ta:] / l[:, ta:])
    ms = jnp.mean(oT * oT, axis=0, keepdims=True)
    oT = oT * lax.rsqrt(ms + LN_EPS)
    o = oT.T * g_ref[...] * (1.0 - lam_init)
    o_ref[...] = o.astype(BF16)


def _attention(slopes, lamp, subln_g, qT, k, vTb, *, B, S, ta, lam_init):
    T = B * S
    nq = S // ta
    assert nq % 2 == 0, "the paired key-block pipeline needs an even number of blocks per sequence"
    vd = V_HEAD_DIM
    kern = functools.partial(_attn_kernel, ta=ta, lam_init=lam_init)
    return pl.pallas_call(
        kern,
        grid=(B, N_HEADS, nq),
        in_specs=[
            pl.BlockSpec(memory_space=pltpu.SMEM),
            _full(lamp.shape),
            _full(subln_g.shape),
            pl.BlockSpec((vd, ta), lambda b, h, i: (h, b * nq + i)),
            pl.BlockSpec((S, vd), lambda b, h, i: (b, h)),
            pl.BlockSpec((nq, vd, ta), lambda b, h, i: (b, h, 0)),
        ],
        out_specs=pl.BlockSpec((ta, vd), lambda b, h, i: (b * nq + i, h)),
        out_shape=jax.ShapeDtypeStruct((T, N_HEADS * vd), BF16),
        scratch_shapes=[
            pltpu.VMEM((1, 2 * ta), F32),
            pltpu.VMEM((1, 2 * ta), F32),
            pltpu.VMEM((vd, 2 * ta), F32),
            pltpu.VMEM((1, 2 * ta), F32),
            pltpu.VMEM((ta, LANES), F32),
            pltpu.VMEM((ta, 2 * ta), F32),
            pltpu.VMEM((ta, 2 * ta), F32),
            pltpu.VMEM((ta, 2 * ta), BF16),
            pltpu.VMEM((ta, 2 * ta), BF16),
        ],
        compiler_params=_cparams(("arbitrary", "arbitrary", "arbitrary")),
        name="diff_attention",
    )(slopes, lamp, subln_g, qT, k, vTb)


def _layer_norm_rows(r, g, b):
    mu = jnp.mean(r, axis=-1, keepdims=True)
    dlt = r - mu
    var = jnp.mean(dlt * dlt, axis=-1, keepdims=True)
    return dlt * lax.rsqrt(var + LN_EPS) * g + b


def _merge_kernel(o_ref, p_ref, ph_ref, gate_ref, x_ref, wap_ref, pw_ref, ps_ref, wpp_ref,
                  wout_ref, g1_ref, b1_ref, h1_ref, ext_ref, *, tm, seq, dn_alpha):
    i = pl.program_id(0)
    t0 = (i * tm) % seq
    hl = POOL_HALO
    ext_ref[0:hl, :] = jnp.where(t0 == 0, 0.0, ph_ref[...])
    ext_ref[hl:hl + tm, :] = p_ref[...]
    pos = (t0 + lax.broadcasted_iota(jnp.int32, (tm, POOL_GROUP_DIM), 0)).astype(F32)
    pm = []
    for gi, w in enumerate(POOL_WINDOWS):
        sl = slice(gi * POOL_GROUP_DIM, (gi + 1) * POOL_GROUP_DIM)
        cur = ext_ref[hl:hl + tm, sl]
        win = cur
        for back in range(1, w):
            win = win + ext_ref[hl - back:hl - back + tm, sl]
        cnt = jnp.minimum(float(w), pos + 1.0)
        pooled = win / cnt - cur
        pm.append(jnp.dot(pooled.astype(BF16), pw_ref[gi], preferred_element_type=F32))
    pm = jnp.concatenate(pm, axis=1) * ps_ref[...]
    pool_branch = jnp.dot(pm.astype(BF16), wpp_ref[...], preferred_element_type=F32)
    attn_branch = jnp.dot(o_ref[...], wap_ref[...], preferred_element_type=F32)
    dm = attn_branch.shape[1]
    gates = gate_ref[...].astype(F32)
    merged = gates[:, :dm] * attn_branch + gates[:, dm:] * pool_branch
    mix = jnp.dot(merged.astype(BF16), wout_ref[...], preferred_element_type=F32)
    h1_ref[...] = _layer_norm_rows(dn_alpha * x_ref[...] + mix, g1_ref[...], b1_ref[...])


def _merge(o, p, gates, x2, wap, pw, ps, wpp, wout, g1, b1, *, tm, seq, dn_alpha):
    T, D = x2.shape
    pwid = p.shape[1]
    hb = tm // POOL_HALO
    kern = functools.partial(_merge_kernel, tm=tm, seq=seq, dn_alpha=dn_alpha)
    return pl.pallas_call(
        kern,
        grid=(T // tm,),
        in_specs=[
            pl.BlockSpec((tm, o.shape[1]), lambda i: (i, 0)),
            pl.BlockSpec((tm, pwid), lambda i: (i, 0)),
            pl.BlockSpec((POOL_HALO, pwid), lambda i: (jnp.maximum(i * hb - 1, 0), 0)),
            pl.BlockSpec((tm, gates.shape[1]), lambda i: (i, 0)),
            pl.BlockSpec((tm, D), lambda i: (i, 0)),
            _full(wap.shape), _full(pw.shape), _full(ps.shape), _full(wpp.shape), _full(wout.shape),
            _full(g1.shape), _full(b1.shape),
        ],
        out_specs=pl.BlockSpec((tm, D), lambda i: (i, 0)),
        out_shape=jax.ShapeDtypeStruct((T, D), F32),
        scratch_shapes=[pltpu.VMEM((POOL_HALO + tm, pwid), F32)],
        compiler_params=_cparams(("arbitrary",)),
        name="merge_ln1",
    )(o, p, p, gates, x2, wap, pw, ps, wpp, wout, g1, b1)


def _top_rows(s, k, payload=None):
    n = s.shape[0]
    row = lax.broadcasted_iota(jnp.int32, s.shape, 0)
    vals, picks = [], []
    for _ in range(k):
        m = jnp.max(s, axis=0, keepdims=True)
        am = jnp.min(jnp.where(s == m, row, n), axis=0, keepdims=True)
        hit = row == am
        vals.append(m)
        if payload is None:
            picks.append(am)
        else:
            picks.append(jnp.sum(jnp.where(hit, payload, 0), axis=0, keepdims=True))
        s = jnp.where(hit, -jnp.inf, s)
    return jnp.concatenate(vals, axis=0), jnp.concatenate(picks, axis=0)


def _route_kernel(y_ref, wq_ref, sk_ref, idx_ref, gate_ref):
    yb = y_ref[...].astype(BF16)
    tm = yb.shape[0]
    nt = (((1,), (1,)), ((), ()))
    kk = PEER_TOPK
    sub = lax.broadcasted_iota(jnp.int32, (SUBLANES, tm), 0)

    def head(h, carry):
        q = jnp.dot(yb, wq_ref[h], preferred_element_type=F32).astype(BF16)
        top = []
        for part in range(2):
            qp = q[:, part * PEER_HALF:(part + 1) * PEER_HALF]
            sT = lax.dot_general(sk_ref[2 * h + part], qp, nt, preferred_element_type=F32)
            top.append(_top_rows(sT, kk))
        (s1, i1), (s2, i2) = top
        cand, eid = [], []
        for a in range(kk):
            nb = kk // (a + 1)
            if nb > SUBLANES:
                cand.append(s1[a:a + 1] + s2)
                eid.append(i1[a:a + 1] * PEER_KEYS + i2)
            else:
                c8 = s1[a:a + 1] + s2[:SUBLANES]
                cand.append(c8 if nb == SUBLANES else jnp.where(sub < nb, c8, -jnp.inf))
                eid.append(i1[a:a + 1] * PEER_KEYS + i2[:SUBLANES])
        sc, ids = _top_rows(jnp.concatenate(cand, axis=0), kk, payload=jnp.concatenate(eid, axis=0))
        e = jnp.exp(sc - sc[0:1])
        gate = e / jnp.sum(e, axis=0, keepdims=True)
        r0 = pl.multiple_of(h * kk, kk)
        idx_ref[pl.ds(r0, kk), :] = ids
        gate_ref[pl.ds(r0, kk), :] = gate
        return carry

    lax.fori_loop(0, PEER_HEADS, head, 0)


def _route(h1, wq3, sk, *, tm):
    T, D = h1.shape
    return pl.pallas_call(
        _route_kernel,
        grid=(T // tm,),
        in_specs=[pl.BlockSpec((tm, D), lambda i: (i, 0)), _full(wq3.shape), _full(sk.shape)],
        out_specs=[pl.BlockSpec((PEER_SLOTS, tm), lambda i: (0, i)),
                   pl.BlockSpec((PEER_SLOTS, tm), lambda i: (0, i))],
        out_shape=[jax.ShapeDtypeStruct((PEER_SLOTS, T), jnp.int32),
                   jax.ShapeDtypeStruct((PEER_SLOTS, T), F32)],
        compiler_params=_cparams(("arbitrary",)),
        name="peer_route",
    )(h1, wq3, sk)


def _pack_table(t):
    n, dd = t.shape
    tb = t.astype(BF16)
    lo = lax.bitcast_convert_type(tb[:, :dd // 2], jnp.uint16).astype(jnp.uint32)
    hi = lax.bitcast_convert_type(tb[:, dd // 2:], jnp.uint16).astype(jnp.uint32)
    return (lo | (hi << 16)).reshape(n * (dd // (2 * LANES)), LANES)


def _unpack_words(w):
    lo = lax.bitcast_convert_type(w << 16, F32)
    hi = lax.bitcast_convert_type(w & jnp.uint32(0xFFFF0000), F32)
    return lo, hi


def _gather_rows(idx_ref, tab_ref, rs, c, dst):
    irow = idx_ref.at[c]
    for e in range(PEER_SLOTS):
        r0 = pl.multiple_of(irow[e], rs)
        dst[pl.ds(e, rs, stride=CHUNK_STRIDE), :] = tab_ref[pl.ds(r0, rs), :]


def _token_pipeline(tb, gather, fold, ca_ref, cb_ref):
    gather(0, ca_ref)

    def pair(kp, carry):
        c0 = 2 * kp
        gather(c0 + 1, cb_ref)
        carry = fold(c0, ca_ref, carry)
        gather(jnp.minimum(c0 + 2, tb - 1), ca_ref)
        return fold(c0 + 1, cb_ref, carry)

    return pair


def _score_kernel(idx_ref, y_ref, gate_ref, tab_ref, w_ref, ca_ref, cb_ref, *, tb):
    nsl = PEER_SLOTS
    rs = y_ref.shape[1] // 2
    st = CHUNK_STRIDE
    half = rs * LANES
    lane = lax.broadcasted_iota(jnp.int32, (nsl, tb), 1)

    def place(c, tot, act):
        return jnp.where(lane == c, jnp.sum(tot, axis=1, keepdims=True), act)

    def fold(c, src, carry):
        act, pend = carry
        act = place(c - 1, pend, act)
        tot = None
        yv = y_ref[c]
        for s in range(rs):
            lo, hi = _unpack_words(src[s * st:s * st + nsl, :])
            part = lo * yv[s:s + 1] + hi * yv[rs + s:rs + s + 1]
            tot = part if tot is None else tot + part
        return act, tot

    gather = functools.partial(_gather_rows, idx_ref, tab_ref, rs)
    pair = _token_pipeline(tb, gather, fold, ca_ref, cb_ref)
    act, pend = lax.fori_loop(0, tb // 2, pair,
                              (jnp.zeros((nsl, tb), F32), jnp.zeros((nsl, LANES), F32)))
    act = place(tb - 1, pend, act)
    gelu = 0.5 * act * (1.0 + lax.erf(act * (2.0 ** -0.5)))
    w_ref[...] = gate_ref[...] * gelu


def _table_spec(tab):
    return pl.BlockSpec(tab.shape, lambda i: (0, 0), pipeline_mode=pl.Buffered(1))


def _chunk_scratch(rs):
    return pltpu.VMEM((CHUNK_STRIDE * rs, LANES), jnp.uint32)


def _expert_scores(idx, y3, gateT, utab, *, tb):
    T = y3.shape[0]
    kern = functools.partial(_score_kernel, tb=tb)
    return pl.pallas_call(
        kern,
        grid=(T // tb,),
        in_specs=[
            pl.BlockSpec((tb, PEER_SLOTS), lambda i: (i, 0), memory_space=pltpu.SMEM),
            pl.BlockSpec((tb,) + y3.shape[1:], lambda i: (i, 0, 0)),
            pl.BlockSpec((PEER_SLOTS, tb), lambda i: (0, i)),
            _table_spec(utab),
        ],
        out_specs=pl.BlockSpec((PEER_SLOTS, tb), lambda i: (0, i)),
        out_shape=jax.ShapeDtypeStruct((PEER_SLOTS, T), F32),
        scratch_shapes=[_chunk_scratch(y3.shape[1] // 2)] * 2,
        compiler_params=_cparams(("arbitrary",)),
        name="peer_scores",
    )(idx, y3, gateT, utab)


def _mix_kernel(idx_ref, w_ref, tab_ref, f_ref, ca_ref, cb_ref, *, tb):
    nsl = PEER_SLOTS
    rs = f_ref.shape[1] // 2
    st = CHUNK_STRIDE
    half = rs * LANES
    lane = lax.broadcasted_iota(jnp.int32, (nsl, tb), 1)

    def fold(c, src, carry):
        wcol = jnp.sum(jnp.where(lane == c, w_ref[...], 0.0), axis=1, keepdims=True)
        wb = jnp.broadcast_to(wcol, (nsl, LANES))
        los, his = [], []
        for s in range(rs):
            lo, hi = _unpack_words(src[s * st:s * st + nsl, :])
            los.append(jnp.sum(lo * wb, axis=0, keepdims=True))
            his.append(jnp.sum(hi * wb, axis=0, keepdims=True))
        f_ref[c] = jnp.concatenate(los + his, axis=0)
        return carry

    gather = functools.partial(_gather_rows, idx_ref, tab_ref, rs)
    pair = _token_pipeline(tb, gather, fold, ca_ref, cb_ref)
    lax.fori_loop(0, tb // 2, pair, 0)


def _expert_mix(idx, wT, vtab, *, tb, rs):
    T = idx.shape[0]
    rows = 2 * rs
    kern = functools.partial(_mix_kernel, tb=tb)
    return pl.pallas_call(
        kern,
        grid=(T // tb,),
        in_specs=[
            pl.BlockSpec((tb, PEER_SLOTS), lambda i: (i, 0), memory_space=pltpu.SMEM),
            pl.BlockSpec((PEER_SLOTS, tb), lambda i: (0, i)),
            _table_spec(vtab),
        ],
        out_specs=pl.BlockSpec((tb, rows, LANES), lambda i: (i, 0, 0)),
        out_shape=jax.ShapeDtypeStruct((T, rows, LANES), F32),
        scratch_shapes=[_chunk_scratch(rs)] * 2,
        compiler_params=_cparams(("arbitrary",)),
        name="peer_mix",
    )(idx, wT, vtab)


def _ln2_kernel(h_ref, f_ref, g_ref, b_ref, o_ref, *, dn_alpha):
    o_ref[...] = _layer_norm_rows(dn_alpha * h_ref[...] + f_ref[...], g_ref[...], b_ref[...])


def _residual_ln(h1, ffn, g, b, *, tm, dn_alpha):
    T, D = h1.shape
    return pl.pallas_call(
        functools.partial(_ln2_kernel, dn_alpha=dn_alpha),
        grid=(T // tm,),
        in_specs=[pl.BlockSpec((tm, D), lambda i: (i, 0)), pl.BlockSpec((tm, D), lambda i: (i, 0)),
                  _full(g.shape), _full(b.shape)],
        out_specs=pl.BlockSpec((tm, D), lambda i: (i, 0)),
        out_shape=jax.ShapeDtypeStruct((T, D), F32),
        compiler_params=_cparams(("arbitrary",)),
        name="residual_ln2",
    )(h1, ffn, g, b)


def _layer(h2d, B, S, depth, lam_init, w_in, b_gate, lq1, lk1, lq2, lk2, subln_g, w_attn_proj, pool_w,
           pool_scale, w_pool_proj, w_out, ln1_g, ln1_b, peer_wq, peer_subkeys, peer_u, peer_v,
           ln2_g, ln2_b):
    T, D = h2d.shape
    dn_alpha = (2.0 * depth) ** 0.25
    wqk = 2 * N_HEADS * HEAD_DIM
    wv = N_HEADS * V_HEAD_DIM
    wpool = len(POOL_WINDOWS) * POOL_GROUP_DIM
    tm = min(PROJ_ROWS, S)
    ta = min(ATTN_BLOCK, S)
    c0, c1, c2, c3 = wqk, 2 * wqk, 2 * wqk + wv, 2 * wqk + wv + wpool
    wb = w_in.astype(BF16)
    row = lambda a: a.reshape(1, -1).astype(F32)

    qT, k, vTb, p, gates = _project(
        h2d, wb[:, :c0].T, wb[:, c0:c1], wb[:, c1:c2].T, wb[:, c2:c3], wb[:, c3:], row(b_gate),
        tm=tm, ta=ta, scale=HEAD_DIM ** -0.5 * LOG2E)

    slopes = jnp.asarray(np.array([2.0 ** (-8.0 * (h + 1) / N_HEADS) * LOG2E for h in range(N_HEADS)],
                                  dtype=np.float32))
    lamp = jnp.stack([lq1, lk1, lq2, lk2]).astype(F32)
    o = _attention(slopes, lamp, row(subln_g), qT, k, vTb, B=B, S=S, ta=ta, lam_init=lam_init)

    h1 = _merge(o, p, gates, h2d, w_attn_proj.astype(BF16), pool_w.astype(BF16), row(pool_scale),
                w_pool_proj.astype(BF16), w_out.astype(BF16), row(ln1_g), row(ln1_b),
                tm=tm, seq=S, dn_alpha=dn_alpha)

    wq3 = peer_wq.astype(BF16).reshape(D, PEER_HEADS, 2 * PEER_HALF).transpose(1, 0, 2)
    sk = peer_subkeys.astype(BF16).reshape(PEER_HEADS * 2, PEER_KEYS, PEER_HALF)
    idxT, gateT = _route(h1, wq3, sk, tm=min(ROUTE_ROWS, T))

    tb = min(PEER_ROWS, T)
    rs = D // (2 * LANES)
    idx = idxT.T * rs
    wT = _expert_scores(idx, h1.reshape(T, D // LANES, LANES), gateT, _pack_table(peer_u), tb=tb)
    ffn = _expert_mix(idx, wT, _pack_table(peer_v), tb=tb, rs=rs)
    return _residual_ln(h1, ffn.reshape(T, D), row(ln2_g), row(ln2_b), tm=tm, dn_alpha=dn_alpha)


def kernel(x, w_in, b_gate, lambda_q1, lambda_k1, lambda_q2, lambda_k2, subln_g, w_attn_proj, pool_w,
           pool_scale, w_pool_proj, w_out, ln1_g, ln1_b, peer_wq, peer_subkeys, peer_u, peer_v,
           ln2_g, ln2_b):
    B, S, D = x.shape
    depth = w_in.shape[0]
    h = x.reshape(B * S, D)
    for l in range(depth):
        lam_init = 0.8 - 0.6 * math.exp(-0.3 * l)
        h = _layer(h, B, S, depth, lam_init, w_in[l], b_gate[l], lambda_q1[l], lambda_k1[l],
                   lambda_q2[l], lambda_k2[l], subln_g[l], w_attn_proj[l], pool_w[l], pool_scale[l],
                   w_pool_proj[l], w_out[l], ln1_g[l], ln1_b[l], peer_wq[l], peer_subkeys[l],
                   peer_u[l], peer_v[l], ln2_g[l], ln2_b[l])
    return h.reshape(B, S, D)
```

```python
import functools
import math

import jax
import jax.numpy as jnp
import numpy as np
from jax import lax
from jax.experimental import pallas as pl
from jax.experimental.pallas import tpu as pltpu

F32 = jnp.float32
BF16 = jnp.bfloat16

N_HEADS = 8
HEAD_DIM = 64
V_HEAD_DIM = 2 * HEAD_DIM
POOL_WINDOWS = (2, 4, 8, 16)
POOL_GROUP_DIM = 128
POOL_HALO = 16
PEER_HEADS = 8
PEER_KEYS = 128
PEER_HALF = 128
PEER_TOPK = 16
PEER_SLOTS = PEER_HEADS * PEER_TOPK
LN_EPS = 1e-5
LOG2E = math.log2(math.e)

LANES = 128
SUBLANES = 8
VMEM_LIMIT_BYTES = 56 * 1024 * 1024
PROJ_ROWS = 512
ATTN_BLOCK = 256
ROUTE_ROWS = 512
PEER_ROWS = 128
CHUNK_STRIDE = PEER_SLOTS + 1
DENOM_ROWS = 16


def _cparams(sem):
    return pltpu.CompilerParams(dimension_semantics=sem, vmem_limit_bytes=VMEM_LIMIT_BYTES)


def _full(shape):
    n = len(shape)
    return pl.BlockSpec(shape, lambda *_: (0,) * n)


def _proj_kernel(x_ref, wqT_ref, wk_ref, wvT_ref, wp_ref, wg_ref, bg_ref,
                 qT_ref, k_ref, vT_ref, p_ref, gate_ref, *, scale, n_vblk, ta):
    xb = x_ref[...].astype(BF16)
    nt = (((1,), (1,)), ((), ()))
    qT = lax.dot_general(wqT_ref[...], xb, nt, preferred_element_type=F32)
    qT_ref[...] = (qT * scale).astype(BF16)
    k_ref[...] = jnp.dot(xb, wk_ref[...], preferred_element_type=F32).astype(BF16)
    vT = lax.dot_general(wvT_ref[...], xb, nt, preferred_element_type=F32).astype(BF16)
    for n in range(n_vblk):
        vT_ref[n] = vT[:, n * ta:(n + 1) * ta]
    p_ref[...] = jnp.dot(xb, wp_ref[...], preferred_element_type=F32)
    g = jnp.dot(xb, wg_ref[...], preferred_element_type=F32) + bg_ref[...]
    gate_ref[...] = jax.nn.sigmoid(g).astype(BF16)


def _project(x2, wqT, wk, wvT, wp, wg, bg, *, tm, ta, scale):
    T, D = x2.shape
    wqk, wv, wpool, wgate = wqT.shape[0], wvT.shape[0], wp.shape[1], wg.shape[1]
    n_vblk = tm // ta
    kern = functools.partial(_proj_kernel, scale=scale, n_vblk=n_vblk, ta=ta)
    return pl.pallas_call(
        kern,
        grid=(T // tm,),
        in_specs=[
            pl.BlockSpec((tm, D), lambda i: (i, 0)),
            _full(wqT.shape), _full(wk.shape), _full(wvT.shape), _full(wp.shape), _full(wg.shape),
            _full(bg.shape),
        ],
        out_specs=[
            pl.BlockSpec((wqk, tm), lambda i: (0, i)),
            pl.BlockSpec((tm, wqk), lambda i: (i, 0)),
            pl.BlockSpec((n_vblk, wv, ta), lambda i: (i, 0, 0)),
            pl.BlockSpec((tm, wpool), lambda i: (i, 0)),
            pl.BlockSpec((tm, wgate), lambda i: (i, 0)),
        ],
        out_shape=[
            jax.ShapeDtypeStruct((wqk, T), BF16),
            jax.ShapeDtypeStruct((T, wqk), BF16),
            jax.ShapeDtypeStruct((T // ta, wv, ta), BF16),
            jax.ShapeDtypeStruct((T, wpool), F32),
            jax.ShapeDtypeStruct((T, wgate), BF16),
        ],
        compiler_params=_cparams(("arbitrary",)),
        name="input_projection",
    )(x2, wqT, wk, wvT, wp, wg, bg)


def _attn_kernel(slopes_ref, lamp_ref, g_ref, qT_ref, k_ref, vT_ref, o_ref,
                 m_ref, acc_ref, al_ref, base_ref, s0_ref, s1_ref, p0_ref, p1_ref,
                 *, tq, tk, lam_init):
    h = pl.program_id(1)
    i = pl.program_id(2)
    slope = slopes_ref[h]
    d = HEAD_DIM
    nrep = 2 * tq // LANES
    qT = qT_ref[...]
    z = jnp.zeros((d, tq), BF16)
    qq = jnp.concatenate([jnp.concatenate([qT[:d], z], axis=0),
                          jnp.concatenate([z, qT[d:]], axis=0)], axis=1)
    m_ref[...] = jnp.full(m_ref.shape, -jnp.inf, F32)
    acc_ref[...] = jnp.zeros(acc_ref.shape, F32)
    al_ref[...] = jnp.ones(al_ref.shape, F32)
    p1_ref[...] = jnp.zeros(p1_ref.shape, BF16)
    base_ref[...] = lax.broadcasted_iota(jnp.int32, (tk, LANES), 0).astype(F32) * slope
    ones_rows = jnp.where(lax.broadcasted_iota(jnp.int32, (DENOM_ROWS, tk), 0) == 0, 1.0, 0.0).astype(BF16)

    def scores(j, s_ref):
        kb = k_ref[pl.ds(pl.multiple_of(j * tk, tk), tk), :]
        s_ref[...] = jnp.dot(kb, qq, preferred_element_type=F32)

    def softmax(j, s_ref, p_ref, masked):
        off = (j * tk - i * tq).astype(F32) * slope
        t = s_ref[...] + jnp.concatenate([base_ref[...]] * nrep, axis=1)
        if masked:
            key = j * tk + lax.broadcasted_iota(jnp.int32, (tk, 2 * tq), 0)
            col = lax.broadcasted_iota(jnp.int32, (tk, 2 * tq), 1)
            qry = i * tq + jnp.where(col >= tq, col - tq, col)
            t = jnp.where(key > qry, -jnp.inf, t)
        m_old = m_ref[...]
        m_new = jnp.maximum(m_old, jnp.max(t, axis=0, keepdims=True) + off)
        alpha = jnp.exp2(m_old - m_new)
        pr = jnp.exp2(t + (off - m_new))
        p_ref[...] = pr.astype(BF16)
        m_ref[...] = m_new
        return alpha

    def values(j, p_ref):
        vb = jnp.concatenate([vT_ref[jnp.maximum(j, 0)], ones_rows], axis=0)
        acc_ref[...] = al_ref[...] * acc_ref[...] + jnp.dot(vb, p_ref[...],
                                                             preferred_element_type=F32)

    def half(j, s_cur, s_nxt, p_cur, p_prv, masked, prefetch):
        if prefetch:
            scores(j + 1, s_nxt)
        alpha = softmax(j, s_cur, p_cur, masked)
        values(j - 1, p_prv)
        al_ref[...] = alpha

    def pair(t, masked, last):
        j0 = 2 * t
        half(j0, s0_ref, s1_ref, p0_ref, p1_ref, masked, True)
        half(j0 + 1, s1_ref, s0_ref, p1_ref, p0_ref, masked, not last)

    scores(0, s0_ref)

    def body(t, carry):
        pair(2 * t, False, False)
        pair(2 * t + 1, False, False)
        return carry

    npairs = (i * tq) // (2 * tk)
    lax.fori_loop(0, npairs // 2, body, 0)

    @pl.when(npairs % 2 == 1)
    def _():
        pair(npairs - 1, False, False)

    pair(npairs, True, True)
    values(2 * npairs + 1, p1_ref)

    lamp = lamp_ref[...]
    lam = (jnp.exp(jnp.sum(lamp[0:1] * lamp[1:2], axis=1, keepdims=True))
           - jnp.exp(jnp.sum(lamp[2:3] * lamp[3:4], axis=1, keepdims=True)) + lam_init)
    acc = acc_ref[0:2 * d, :]
    l = acc_ref[2 * d:2 * d + 1, :]
    oT = acc[:, :tq] / l[:, :tq] - lam * (acc[:, tq:] / l[:, tq:])
    ms = jnp.mean(oT * oT, axis=0, keepdims=True)
    oT = oT * lax.rsqrt(ms + LN_EPS)
    o = oT.T * g_ref[...] * (1.0 - lam_init)
    o_ref[...] = o.astype(BF16)


def _attention(slopes, lamp, subln_g, qT, k, vTb, *, B, S, tq, tk, lam_init):
    T = B * S
    nq = S // tq
    nk = S // tk
    assert tq in (tk, 2 * tk) and nk % 2 == 0
    vd = V_HEAD_DIM
    kern = functools.partial(_attn_kernel, tq=tq, tk=tk, lam_init=lam_init)
    return pl.pallas_call(
        kern,
        grid=(B, N_HEADS, nq),
        in_specs=[
            pl.BlockSpec(memory_space=pltpu.SMEM),
            _full(lamp.shape),
            _full(subln_g.shape),
            pl.BlockSpec((vd, tq), lambda b, h, i: (h, b * nq + i)),
            pl.BlockSpec((S, vd), lambda b, h, i: (b, h)),
            pl.BlockSpec((nk, vd, tk), lambda b, h, i: (b, h, 0)),
        ],
        out_specs=pl.BlockSpec((tq, vd), lambda b, h, i: (b * nq + i, h)),
        out_shape=jax.ShapeDtypeStruct((T, N_HEADS * vd), BF16),
        scratch_shapes=[
            pltpu.VMEM((1, 2 * tq), F32),
            pltpu.VMEM((vd + DENOM_ROWS, 2 * tq), F32),
            pltpu.VMEM((1, 2 * tq), F32),
            pltpu.VMEM((tk, LANES), F32),
            pltpu.VMEM((tk, 2 * tq), F32),
            pltpu.VMEM((tk, 2 * tq), F32),
            pltpu.VMEM((tk, 2 * tq), BF16),
            pltpu.VMEM((tk, 2 * tq), BF16),
        ],
        compiler_params=_cparams(("arbitrary", "arbitrary", "arbitrary")),
        name="diff_attention",
    )(slopes, lamp, subln_g, qT, k, vTb)


def _layer_norm_rows(r, g, b):
    mu = jnp.mean(r, axis=-1, keepdims=True)
    dlt = r - mu
    var = jnp.mean(dlt * dlt, axis=-1, keepdims=True)
    return dlt * lax.rsqrt(var + LN_EPS) * g + b


def _merge_kernel(o_ref, p_ref, ph_ref, gate_ref, x_ref, wap_ref, pw_ref, ps_ref, wpp_ref,
                  wout_ref, g1_ref, b1_ref, h1_ref, ext_ref, *, tm, seq, dn_alpha):
    i = pl.program_id(0)
    t0 = (i * tm) % seq
    hl = POOL_HALO
    ext_ref[0:hl, :] = jnp.where(t0 == 0, 0.0, ph_ref[...])
    ext_ref[hl:hl + tm, :] = p_ref[...]
    pos = (t0 + lax.broadcasted_iota(jnp.int32, (tm, POOL_GROUP_DIM), 0)).astype(F32)
    pm = []
    for gi, w in enumerate(POOL_WINDOWS):
        sl = slice(gi * POOL_GROUP_DIM, (gi + 1) * POOL_GROUP_DIM)
        cur = ext_ref[hl:hl + tm, sl]
        win = cur
        for back in range(1, w):
            win = win + ext_ref[hl - back:hl - back + tm, sl]
        cnt = jnp.minimum(float(w), pos + 1.0)
        pooled = win / cnt - cur
        pm.append(jnp.dot(pooled.astype(BF16), pw_ref[gi], preferred_element_type=F32))
    pm = jnp.concatenate(pm, axis=1) * ps_ref[...]
    pool_branch = jnp.dot(pm.astype(BF16), wpp_ref[...], preferred_element_type=F32)
    attn_branch = jnp.dot(o_ref[...], wap_ref[...], preferred_element_type=F32)
    dm = attn_branch.shape[1]
    gates = gate_ref[...].astype(F32)
    merged = gates[:, :dm] * attn_branch + gates[:, dm:] * pool_branch
    mix = jnp.dot(merged.astype(BF16), wout_ref[...], preferred_element_type=F32)
    h1_ref[...] = _layer_norm_rows(dn_alpha * x_ref[...] + mix, g1_ref[...], b1_ref[...])


def _merge(o, p, gates, x2, wap, pw, ps, wpp, wout, g1, b1, *, tm, seq, dn_alpha):
    T, D = x2.shape
    pwid = p.shape[1]
    hb = tm // POOL_HALO
    kern = functools.partial(_merge_kernel, tm=tm, seq=seq, dn_alpha=dn_alpha)
    return pl.pallas_call(
        kern,
        grid=(T // tm,),
        in_specs=[
            pl.BlockSpec((tm, o.shape[1]), lambda i: (i, 0)),
            pl.BlockSpec((tm, pwid), lambda i: (i, 0)),
            pl.BlockSpec((POOL_HALO, pwid), lambda i: (jnp.maximum(i * hb - 1, 0), 0)),
            pl.BlockSpec((tm, gates.shape[1]), lambda i: (i, 0)),
            pl.BlockSpec((tm, D), lambda i: (i, 0)),
            _full(wap.shape), _full(pw.shape), _full(ps.shape), _full(wpp.shape), _full(wout.shape),
            _full(g1.shape), _full(b1.shape),
        ],
        out_specs=pl.BlockSpec((tm, D), lambda i: (i, 0)),
        out_shape=jax.ShapeDtypeStruct((T, D), F32),
        scratch_shapes=[pltpu.VMEM((POOL_HALO + tm, pwid), F32)],
        compiler_params=_cparams(("arbitrary",)),
        name="merge_ln1",
    )(o, p, p, gates, x2, wap, pw, ps, wpp, wout, g1, b1)


def _top_rows(s, k, payload=None):
    n = s.shape[0]
    row = lax.broadcasted_iota(jnp.int32, s.shape, 0)
    vals, picks = [], []
    for _ in range(k):
        m = jnp.max(s, axis=0, keepdims=True)
        am = jnp.min(jnp.where(s == m, row, n), axis=0, keepdims=True)
        hit = row == am
        vals.append(m)
        if payload is None:
            picks.append(am)
        else:
            picks.append(jnp.sum(jnp.where(hit, payload, 0), axis=0, keepdims=True))
        s = jnp.where(hit, -jnp.inf, s)
    return jnp.concatenate(vals, axis=0), jnp.concatenate(picks, axis=0)


def _route_kernel(y_ref, wq_ref, sk_ref, idx_ref, gate_ref):
    yb = y_ref[...].astype(BF16)
    tm = yb.shape[0]
    nt = (((1,), (1,)), ((), ()))
    kk = PEER_TOPK
    sub = lax.broadcasted_iota(jnp.int32, (SUBLANES, tm), 0)

    def head(h, carry):
        q = jnp.dot(yb, wq_ref[h], preferred_element_type=F32).astype(BF16)
        top = []
        for part in range(2):
            qp = q[:, part * PEER_HALF:(part + 1) * PEER_HALF]
            sT = lax.dot_general(sk_ref[2 * h + part], qp, nt, preferred_element_type=F32)
            top.append(_top_rows(sT, kk))
        (s1, i1), (s2, i2) = top
        cand, eid = [], []
        for a in range(kk):
            nb = kk // (a + 1)
            if nb > SUBLANES:
                cand.append(s1[a:a + 1] + s2)
                eid.append(i1[a:a + 1] * PEER_KEYS + i2)
            else:
                c8 = s1[a:a + 1] + s2[:SUBLANES]
                cand.append(c8 if nb == SUBLANES else jnp.where(sub < nb, c8, -jnp.inf))
                eid.append(i1[a:a + 1] * PEER_KEYS + i2[:SUBLANES])
        sc, ids = _top_rows(jnp.concatenate(cand, axis=0), kk, payload=jnp.concatenate(eid, axis=0))
        e = jnp.exp(sc - sc[0:1])
        gate = e / jnp.sum(e, axis=0, keepdims=True)
        r0 = pl.multiple_of(h * kk, kk)
        idx_ref[pl.ds(r0, kk), :] = ids
        gate_ref[pl.ds(r0, kk), :] = gate
        return carry

    lax.fori_loop(0, PEER_HEADS, head, 0)


def _route(h1, wq3, sk, *, tm):
    T, D = h1.shape
    return pl.pallas_call(
        _route_kernel,
        grid=(T // tm,),
        in_specs=[pl.BlockSpec((tm, D), lambda i: (i, 0)), _full(wq3.shape), _full(sk.shape)],
        out_specs=[pl.BlockSpec((PEER_SLOTS, tm), lambda i: (0, i)),
                   pl.BlockSpec((PEER_SLOTS, tm), lambda i: (0, i))],
        out_shape=[jax.ShapeDtypeStruct((PEER_SLOTS, T), jnp.int32),
                   jax.ShapeDtypeStruct((PEER_SLOTS, T), F32)],
        compiler_params=_cparams(("arbitrary",)),
        name="peer_route",
    )(h1, wq3, sk)


def _pack_table(t):
    n, dd = t.shape
    tb = t.astype(BF16)
    lo = lax.bitcast_convert_type(tb[:, :dd // 2], jnp.uint16).astype(jnp.uint32)
    hi = lax.bitcast_convert_type(tb[:, dd // 2:], jnp.uint16).astype(jnp.uint32)
    return (lo | (hi << 16)).reshape(n * (dd // (2 * LANES)), LANES)


def _unpack_words(w):
    lo = lax.bitcast_convert_type(w << 16, F32)
    hi = lax.bitcast_convert_type(w & jnp.uint32(0xFFFF0000), F32)
    return lo, hi


def _gather_rows(idx_ref, tab_ref, rs, c, dst):
    irow = idx_ref.at[c]
    for e in range(PEER_SLOTS):
        r0 = pl.multiple_of(irow[e], rs)
        dst[pl.ds(e, rs, stride=CHUNK_STRIDE), :] = tab_ref[pl.ds(r0, rs), :]


def _token_pipeline(tb, gather, fold, ca_ref, cb_ref):
    gather(0, ca_ref)

    def pair(kp, carry):
        c0 = 2 * kp
        gather(c0 + 1, cb_ref)
        carry = fold(c0, ca_ref, carry)
        gather(jnp.minimum(c0 + 2, tb - 1), ca_ref)
        return fold(c0 + 1, cb_ref, carry)

    return pair


def _score_kernel(idx_ref, y_ref, gate_ref, tab_ref, w_ref, ca_ref, cb_ref, *, tb):
    nsl = PEER_SLOTS
    rs = y_ref.shape[1] // 2
    st = CHUNK_STRIDE
    half = rs * LANES
    lane = lax.broadcasted_iota(jnp.int32, (nsl, tb), 1)

    def place(c, tot, act):
        return jnp.where(lane == c, jnp.sum(tot, axis=1, keepdims=True), act)

    def fold(c, src, carry):
        act, pend = carry
        act = place(c - 1, pend, act)
        tot = None
        yv = y_ref[c]
        for s in range(rs):
            lo, hi = _unpack_words(src[s * st:s * st + nsl, :])
            part = lo * yv[s:s + 1] + hi * yv[rs + s:rs + s + 1]
            tot = part if tot is None else tot + part
        return act, tot

    gather = functools.partial(_gather_rows, idx_ref, tab_ref, rs)
    pair = _token_pipeline(tb, gather, fold, ca_ref, cb_ref)
    act, pend = lax.fori_loop(0, tb // 2, pair,
                              (jnp.zeros((nsl, tb), F32), jnp.zeros((nsl, LANES), F32)))
    act = place(tb - 1, pend, act)
    gelu = 0.5 * act * (1.0 + lax.erf(act * (2.0 ** -0.5)))
    w_ref[...] = gate_ref[...] * gelu


def _table_spec(tab):
    return pl.BlockSpec(tab.shape, lambda i: (0, 0), pipeline_mode=pl.Buffered(1))


def _chunk_scratch(rs):
    return pltpu.VMEM((CHUNK_STRIDE * rs, LANES), jnp.uint32)


def _expert_scores(idx, y3, gateT, utab, *, tb):
    T = y3.shape[0]
    kern = functools.partial(_score_kernel, tb=tb)
    return pl.pallas_call(
        kern,
        grid=(T // tb,),
        in_specs=[
            pl.BlockSpec((tb, PEER_SLOTS), lambda i: (i, 0), memory_space=pltpu.SMEM),
            pl.BlockSpec((tb,) + y3.shape[1:], lambda i: (i, 0, 0)),
            pl.BlockSpec((PEER_SLOTS, tb), lambda i: (0, i)),
            _table_spec(utab),
        ],
        out_specs=pl.BlockSpec((PEER_SLOTS, tb), lambda i: (0, i)),
        out_shape=jax.ShapeDtypeStruct((PEER_SLOTS, T), F32),
        scratch_shapes=[_chunk_scratch(y3.shape[1] // 2)] * 2,
        compiler_params=_cparams(("arbitrary",)),
        name="peer_scores",
    )(idx, y3, gateT, utab)


def _mix_kernel(idx_ref, w_ref, tab_ref, f_ref, ca_ref, cb_ref, *, tb):
    nsl = PEER_SLOTS
    rs = f_ref.shape[1] // 2
    st = CHUNK_STRIDE
    half = rs * LANES
    lane = lax.broadcasted_iota(jnp.int32, (nsl, tb), 1)

    def fold(c, src, carry):
        wcol = jnp.sum(jnp.where(lane == c, w_ref[...], 0.0), axis=1, keepdims=True)
        wb = jnp.broadcast_to(wcol, (nsl, LANES))
        los, his = [], []
        for s in range(rs):
            lo, hi = _unpack_words(src[s * st:s * st + nsl, :])
            los.append(jnp.sum(lo * wb, axis=0, keepdims=True))
            his.append(jnp.sum(hi * wb, axis=0, keepdims=True))
        f_ref[c] = jnp.concatenate(los + his, axis=0)
        return carry

    gather = functools.partial(_gather_rows, idx_ref, tab_ref, rs)
    pair = _token_pipeline(tb, gather, fold, ca_ref, cb_ref)
    lax.fori_loop(0, tb // 2, pair, 0)


def _expert_mix(idx, wT, vtab, *, tb, rs):
    T = idx.shape[0]
    rows = 2 * rs
    kern = functools.partial(_mix_kernel, tb=tb)
    return pl.pallas_call(
        kern,
        grid=(T // tb,),
        in_specs=[
            pl.BlockSpec((tb, PEER_SLOTS), lambda i: (i, 0), memory_space=pltpu.SMEM),
            pl.BlockSpec((PEER_SLOTS, tb), lambda i: (0, i)),
            _table_spec(vtab),
        ],
        out_specs=pl.BlockSpec((tb, rows, LANES), lambda i: (i, 0, 0)),
        out_shape=jax.ShapeDtypeStruct((T, rows, LANES), F32),
        scratch_shapes=[_chunk_scratch(rs)] * 2,
        compiler_params=_cparams(("arbitrary",)),
        name="peer_mix",
    )(idx, wT, vtab)


def _ln2_kernel(h_ref, f_ref, g_ref, b_ref, o_ref, *, dn_alpha):
    o_ref[...] = _layer_norm_rows(dn_alpha * h_ref[...] + f_ref[...], g_ref[...], b_ref[...])


def _residual_ln(h1, ffn, g, b, *, tm, dn_alpha):
    T, D = h1.shape
    return pl.pallas_call(
        functools.partial(_ln2_kernel, dn_alpha=dn_alpha),
        grid=(T // tm,),
        in_specs=[pl.BlockSpec((tm, D), lambda i: (i, 0)), pl.BlockSpec((tm, D), lambda i: (i, 0)),
                  _full(g.shape), _full(b.shape)],
        out_specs=pl.BlockSpec((tm, D), lambda i: (i, 0)),
        out_shape=jax.ShapeDtypeStruct((T, D), F32),
        compiler_params=_cparams(("arbitrary",)),
        name="residual_ln2",
    )(h1, ffn, g, b)


def _layer(h2d, B, S, depth, lam_init, w_in, b_gate, lq1, lk1, lq2, lk2, subln_g, w_attn_proj, pool_w,
           pool_scale, w_pool_proj, w_out, ln1_g, ln1_b, peer_wq, peer_subkeys, peer_u, peer_v,
           ln2_g, ln2_b):
    T, D = h2d.shape
    dn_alpha = (2.0 * depth) ** 0.25
    wqk = 2 * N_HEADS * HEAD_DIM
    wv = N_HEADS * V_HEAD_DIM
    wpool = len(POOL_WINDOWS) * POOL_GROUP_DIM
    tm = min(PROJ_ROWS, S)
    ta = min(ATTN_BLOCK, S // 2)
    c0, c1, c2, c3 = wqk, 2 * wqk, 2 * wqk + wv, 2 * wqk + wv + wpool
    wb = w_in.astype(BF16)
    row = lambda a: a.reshape(1, -1).astype(F32)

    qT, k, vTb, p, gates = _project(
        h2d, wb[:, :c0].T, wb[:, c0:c1], wb[:, c1:c2].T, wb[:, c2:c3], wb[:, c3:], row(b_gate),
        tm=tm, ta=ta, scale=HEAD_DIM ** -0.5 * LOG2E)

    slopes = jnp.asarray(np.array([2.0 ** (-8.0 * (h + 1) / N_HEADS) * LOG2E for h in range(N_HEADS)],
                                  dtype=np.float32))
    lamp = jnp.stack([lq1, lk1, lq2, lk2]).astype(F32)
    o = _attention(slopes, lamp, row(subln_g), qT, k, vTb, B=B, S=S, tq=ta, tk=ta, lam_init=lam_init)

    h1 = _merge(o, p, gates, h2d, w_attn_proj.astype(BF16), pool_w.astype(BF16), row(pool_scale),
                w_pool_proj.astype(BF16), w_out.astype(BF16), row(ln1_g), row(ln1_b),
                tm=tm, seq=S, dn_alpha=dn_alpha)

    wq3 = peer_wq.astype(BF16).reshape(D, PEER_HEADS, 2 * PEER_HALF).transpose(1, 0, 2)
    sk = peer_subkeys.astype(BF16).reshape(PEER_HEADS * 2, PEER_KEYS, PEER_HALF)
    idxT, gateT = _route(h1, wq3, sk, tm=min(ROUTE_ROWS, T))

    tb = min(PEER_ROWS, T)
    rs = D // (2 * LANES)
    idx = idxT.T * rs
    wT = _expert_scores(idx, h1.reshape(T, D // LANES, LANES), gateT, _pack_table(peer_u), tb=tb)
    ffn = _expert_mix(idx, wT, _pack_table(peer_v), tb=tb, rs=rs)
    return _residual_ln(h1, ffn.reshape(T, D), row(ln2_g), row(ln2_b), tm=tm, dn_alpha=dn_alpha)


def kernel(x, w_in, b_gate, lambda_q1, lambda_k1, lambda_q2, lambda_k2, subln_g, w_attn_proj, pool_w,
           pool_scale, w_pool_proj, w_out, ln1_g, ln1_b, peer_wq, peer_subkeys, peer_u, peer_v,
           ln2_g, ln2_b):
    B, S, D = x.shape
    depth = w_in.shape[0]
    h = x.reshape(B * S, D)
    for l in range(depth):
        lam_init = 0.8 - 0.6 * math.exp(-0.3 * l)
        h = _layer(h, B, S, depth, lam_init, w_in[l], b_gate[l], lambda_q1[l], lambda_k1[l],
                   lambda_q2[l], lambda_k2[l], subln_g[l], w_attn_proj[l], pool_w[l], pool_scale[l],
                   w_pool_proj[l], w_out[l], ln1_g[l], ln1_b[l], peer_wq[l], peer_subkeys[l],
                   peer_u[l], peer_v[l], ln2_g[l], ln2_b[l])
    return h.reshape(B, S, D)
```

```python
import functools
import math

import jax
import jax.numpy as jnp
import numpy as np
from jax import lax
from jax.experimental import pallas as pl
from jax.experimental.pallas import tpu as pltpu

F32 = jnp.float32
BF16 = jnp.bfloat16

N_HEADS = 8
HEAD_DIM = 64
V_HEAD_DIM = 2 * HEAD_DIM
POOL_WINDOWS = (2, 4, 8, 16)
POOL_GROUP_DIM = 128
POOL_HALO = 16
PEER_HEADS = 8
PEER_KEYS = 128
PEER_HALF = 128
PEER_TOPK = 16
PEER_SLOTS = PEER_HEADS * PEER_TOPK
LN_EPS = 1e-5
LOG2E = math.log2(math.e)

LANES = 128
SUBLANES = 8
VMEM_LIMIT_BYTES = 56 * 1024 * 1024
PROJ_ROWS = 512
ATTN_BLOCK = 256
ROUTE_ROWS = 1024
PEER_ROWS = 128
CHUNK_STRIDE = PEER_SLOTS + 1
DENOM_ROWS = 16
PAIRS_PER_TRIP = 4


def _cparams(sem):
    return pltpu.CompilerParams(dimension_semantics=sem, vmem_limit_bytes=VMEM_LIMIT_BYTES)


def _full(shape):
    n = len(shape)
    return pl.BlockSpec(shape, lambda *_: (0,) * n)


def _proj_kernel(x_ref, wqT_ref, wk_ref, wvT_ref, wp_ref, wg_ref, bg_ref,
                 qT_ref, k_ref, vT_ref, p_ref, gate_ref, *, scale, n_vblk, ta):
    xb = x_ref[...].astype(BF16)
    nt = (((1,), (1,)), ((), ()))
    qT = lax.dot_general(wqT_ref[...], xb, nt, preferred_element_type=F32)
    qT_ref[...] = (qT * scale).astype(BF16)
    k_ref[...] = jnp.dot(xb, wk_ref[...], preferred_element_type=F32).astype(BF16)
    vT = lax.dot_general(wvT_ref[...], xb, nt, preferred_element_type=F32).astype(BF16)
    for n in range(n_vblk):
        vT_ref[n] = vT[:, n * ta:(n + 1) * ta]
    p_ref[...] = jnp.dot(xb, wp_ref[...], preferred_element_type=F32)
    g = jnp.dot(xb, wg_ref[...], preferred_element_type=F32) + bg_ref[...]
    gate_ref[...] = jax.nn.sigmoid(g).astype(BF16)


def _project(x2, wqT, wk, wvT, wp, wg, bg, *, tm, ta, scale):
    T, D = x2.shape
    wqk, wv, wpool, wgate = wqT.shape[0], wvT.shape[0], wp.shape[1], wg.shape[1]
    n_vblk = tm // ta
    kern = functools.partial(_proj_kernel, scale=scale, n_vblk=n_vblk, ta=ta)
    return pl.pallas_call(
        kern,
        grid=(T // tm,),
        in_specs=[
            pl.BlockSpec((tm, D), lambda i: (i, 0)),
            _full(wqT.shape), _full(wk.shape), _full(wvT.shape), _full(wp.shape), _full(wg.shape),
            _full(bg.shape),
        ],
        out_specs=[
            pl.BlockSpec((wqk, tm), lambda i: (0, i)),
            pl.BlockSpec((tm, wqk), lambda i: (i, 0)),
            pl.BlockSpec((n_vblk, wv, ta), lambda i: (i, 0, 0)),
            pl.BlockSpec((tm, wpool), lambda i: (i, 0)),
            pl.BlockSpec((tm, wgate), lambda i: (i, 0)),
        ],
        out_shape=[
            jax.ShapeDtypeStruct((wqk, T), BF16),
            jax.ShapeDtypeStruct((T, wqk), BF16),
            jax.ShapeDtypeStruct((T // ta, wv, ta), BF16),
            jax.ShapeDtypeStruct((T, wpool), F32),
            jax.ShapeDtypeStruct((T, wgate), BF16),
        ],
        compiler_params=_cparams(("arbitrary",)),
        name="input_projection",
    )(x2, wqT, wk, wvT, wp, wg, bg)


def _attn_kernel(slopes_ref, lamp_ref, g_ref, qT_ref, k_ref, vT_ref, o_ref,
                 m_ref, acc_ref, al_ref, base_ref, s0_ref, s1_ref, p0_ref, p1_ref,
                 *, tq, tk, lam_init):
    h = pl.program_id(1)
    i = pl.program_id(2)
    slope = slopes_ref[h]
    d = HEAD_DIM
    nrep = 2 * tq // LANES
    qT = qT_ref[...]
    z = jnp.zeros((d, tq), BF16)
    qq = jnp.concatenate([jnp.concatenate([qT[:d], z], axis=0),
                          jnp.concatenate([z, qT[d:]], axis=0)], axis=1)
    m_ref[...] = jnp.full(m_ref.shape, -jnp.inf, F32)
    acc_ref[...] = jnp.zeros(acc_ref.shape, F32)
    al_ref[...] = jnp.ones(al_ref.shape, F32)
    p1_ref[...] = jnp.zeros(p1_ref.shape, BF16)
    base_ref[...] = lax.broadcasted_iota(jnp.int32, (tk, LANES), 0).astype(F32) * slope
    ones_rows = jnp.where(lax.broadcasted_iota(jnp.int32, (DENOM_ROWS, tk), 0) == 0, 1.0, 0.0).astype(BF16)

    def scores(j, s_ref):
        kb = k_ref[pl.ds(pl.multiple_of(j * tk, tk), tk), :]
        s_ref[...] = jnp.dot(kb, qq, preferred_element_type=F32)

    def softmax(j, s_ref, p_ref, masked):
        off = (j * tk - i * tq).astype(F32) * slope
        t = s_ref[...] + jnp.concatenate([base_ref[...]] * nrep, axis=1)
        if masked:
            key = j * tk + lax.broadcasted_iota(jnp.int32, (tk, 2 * tq), 0)
            col = lax.broadcasted_iota(jnp.int32, (tk, 2 * tq), 1)
            qry = i * tq + jnp.where(col >= tq, col - tq, col)
            t = jnp.where(key > qry, -jnp.inf, t)
        m_old = m_ref[...]
        m_new = jnp.maximum(m_old, jnp.max(t, axis=0, keepdims=True) + off)
        alpha = jnp.exp2(m_old - m_new)
        pr = jnp.exp2(t + (off - m_new))
        p_ref[...] = pr.astype(BF16)
        m_ref[...] = m_new
        return alpha

    def values(j, p_ref):
        vb = jnp.concatenate([vT_ref[jnp.maximum(j, 0)], ones_rows], axis=0)
        acc_ref[...] = al_ref[...] * acc_ref[...] + jnp.dot(vb, p_ref[...],
                                                             preferred_element_type=F32)

    def half(j, s_cur, s_nxt, p_cur, p_prv, masked, prefetch):
        if prefetch:
            scores(j + 1, s_nxt)
        alpha = softmax(j, s_cur, p_cur, masked)
        values(j - 1, p_prv)
        al_ref[...] = alpha

    def pair(t, masked, last):
        j0 = 2 * t
        half(j0, s0_ref, s1_ref, p0_ref, p1_ref, masked, True)
        half(j0 + 1, s1_ref, s0_ref, p1_ref, p0_ref, masked, not last)

    scores(0, s0_ref)

    def body(t, carry):
        for u in range(PAIRS_PER_TRIP):
            pair(PAIRS_PER_TRIP * t + u, False, False)
        return carry

    npairs = (i * tq) // (2 * tk)
    ntrips = npairs // PAIRS_PER_TRIP
    lax.fori_loop(0, ntrips, body, 0)
    done = ntrips * PAIRS_PER_TRIP
    step = PAIRS_PER_TRIP // 2
    while step:
        take = ((npairs - done) & step) != 0

        @pl.when(take)
        def _(done=done, step=step):
            for u in range(step):
                pair(done + u, False, False)

        done = done + jnp.where(take, step, 0)
        step //= 2

    pair(npairs, True, True)
    values(2 * npairs + 1, p1_ref)

    lamp = lamp_ref[...]
    lam = (jnp.exp(jnp.sum(lamp[0:1] * lamp[1:2], axis=1, keepdims=True))
           - jnp.exp(jnp.sum(lamp[2:3] * lamp[3:4], axis=1, keepdims=True)) + lam_init)
    acc = acc_ref[0:2 * d, :]
    l = acc_ref[2 * d:2 * d + 1, :]
    oT = acc[:, :tq] / l[:, :tq] - lam * (acc[:, tq:] / l[:, tq:])
    ms = jnp.mean(oT * oT, axis=0, keepdims=True)
    oT = oT * lax.rsqrt(ms + LN_EPS)
    o = oT.T * g_ref[...] * (1.0 - lam_init)
    o_ref[...] = o.astype(BF16)


def _attention(slopes, lamp, subln_g, qT, k, vTb, *, B, S, tq, tk, lam_init):
    T = B * S
    nq = S // tq
    nk = S // tk
    assert tq in (tk, 2 * tk) and nk % 2 == 0
    vd = V_HEAD_DIM
    kern = functools.partial(_attn_kernel, tq=tq, tk=tk, lam_init=lam_init)
    return pl.pallas_call(
        kern,
        grid=(B, N_HEADS, nq),
        in_specs=[
            pl.BlockSpec(memory_space=pltpu.SMEM),
            _full(lamp.shape),
            _full(subln_g.shape),
            pl.BlockSpec((vd, tq), lambda b, h, i: (h, b * nq + i)),
            pl.BlockSpec((S, vd), lambda b, h, i: (b, h)),
            pl.BlockSpec((nk, vd, tk), lambda b, h, i: (b, h, 0)),
        ],
        out_specs=pl.BlockSpec((tq, vd), lambda b, h, i: (b * nq + i, h)),
        out_shape=jax.ShapeDtypeStruct((T, N_HEADS * vd), BF16),
        scratch_shapes=[
            pltpu.VMEM((1, 2 * tq), F32),
            pltpu.VMEM((vd + DENOM_ROWS, 2 * tq), F32),
            pltpu.VMEM((1, 2 * tq), F32),
            pltpu.VMEM((tk, LANES), F32),
            pltpu.VMEM((tk, 2 * tq), F32),
            pltpu.VMEM((tk, 2 * tq), F32),
            pltpu.VMEM((tk, 2 * tq), BF16),
            pltpu.VMEM((tk, 2 * tq), BF16),
        ],
        compiler_params=_cparams(("arbitrary", "arbitrary", "arbitrary")),
        name="diff_attention",
    )(slopes, lamp, subln_g, qT, k, vTb)


def _layer_norm_rows(r, g, b):
    mu = jnp.mean(r, axis=-1, keepdims=True)
    dlt = r - mu
    var = jnp.mean(dlt * dlt, axis=-1, keepdims=True)
    return dlt * lax.rsqrt(var + LN_EPS) * g + b


def _merge_kernel(o_ref, p_ref, ph_ref, gate_ref, x_ref, wap_ref, pw_ref, ps_ref, wpp_ref,
                  wout_ref, g1_ref, b1_ref, h1_ref, ext_ref, *, tm, seq, dn_alpha):
    i = pl.program_id(0)
    t0 = (i * tm) % seq
    hl = POOL_HALO
    ext_ref[0:hl, :] = jnp.where(t0 == 0, 0.0, ph_ref[...])
    ext_ref[hl:hl + tm, :] = p_ref[...]
    pos = (t0 + lax.broadcasted_iota(jnp.int32, (tm, POOL_GROUP_DIM), 0)).astype(F32)
    pm = []
    for gi, w in enumerate(POOL_WINDOWS):
        sl = slice(gi * POOL_GROUP_DIM, (gi + 1) * POOL_GROUP_DIM)
        cur = ext_ref[hl:hl + tm, sl]
        win = cur
        for back in range(1, w):
            win = win + ext_ref[hl - back:hl - back + tm, sl]
        cnt = jnp.minimum(float(w), pos + 1.0)
        pooled = win / cnt - cur
        pm.append(jnp.dot(pooled.astype(BF16), pw_ref[gi], preferred_element_type=F32))
    pm = jnp.concatenate(pm, axis=1) * ps_ref[...]
    pool_branch = jnp.dot(pm.astype(BF16), wpp_ref[...], preferred_element_type=F32)
    attn_branch = jnp.dot(o_ref[...], wap_ref[...], preferred_element_type=F32)
    dm = attn_branch.shape[1]
    gates = gate_ref[...].astype(F32)
    merged = gates[:, :dm] * attn_branch + gates[:, dm:] * pool_branch
    mix = jnp.dot(merged.astype(BF16), wout_ref[...], preferred_element_type=F32)
    h1_ref[...] = _layer_norm_rows(dn_alpha * x_ref[...] + mix, g1_ref[...], b1_ref[...])


def _merge(o, p, gates, x2, wap, pw, ps, wpp, wout, g1, b1, *, tm, seq, dn_alpha):
    T, D = x2.shape
    pwid = p.shape[1]
    hb = tm // POOL_HALO
    kern = functools.partial(_merge_kernel, tm=tm, seq=seq, dn_alpha=dn_alpha)
    return pl.pallas_call(
        kern,
        grid=(T // tm,),
        in_specs=[
            pl.BlockSpec((tm, o.shape[1]), lambda i: (i, 0)),
            pl.BlockSpec((tm, pwid), lambda i: (i, 0)),
            pl.BlockSpec((POOL_HALO, pwid), lambda i: (jnp.maximum(i * hb - 1, 0), 0)),
            pl.BlockSpec((tm, gates.shape[1]), lambda i: (i, 0)),
            pl.BlockSpec((tm, D), lambda i: (i, 0)),
            _full(wap.shape), _full(pw.shape), _full(ps.shape), _full(wpp.shape), _full(wout.shape),
            _full(g1.shape), _full(b1.shape),
        ],
        out_specs=pl.BlockSpec((tm, D), lambda i: (i, 0)),
        out_shape=jax.ShapeDtypeStruct((T, D), F32),
        scratch_shapes=[pltpu.VMEM((POOL_HALO + tm, pwid), F32)],
        compiler_params=_cparams(("arbitrary",)),
        name="merge_ln1",
    )(o, p, p, gates, x2, wap, pw, ps, wpp, wout, g1, b1)


def _top_rows(s, k, payload=None):
    n = s.shape[0]
    row = lax.broadcasted_iota(jnp.int32, s.shape, 0)
    vals, picks = [], []
    for _ in range(k):
        m = jnp.max(s, axis=0, keepdims=True)
        am = jnp.min(jnp.where(s == m, row, n), axis=0, keepdims=True)
        hit = row == am
        vals.append(m)
        if payload is None:
            picks.append(am)
        else:
            picks.append(jnp.sum(jnp.where(hit, payload, 0), axis=0, keepdims=True))
        s = jnp.where(hit, -jnp.inf, s)
    return jnp.concatenate(vals, axis=0), jnp.concatenate(picks, axis=0)


def _packed_candidates(s1, i1, s2, i2, sub):
    kk = s1.shape[0]
    segs, start = [], 0
    for a in range(kk):
        segs.append((a, start, kk // (a + 1)))
        start += kk // (a + 1)
    total = start
    cand, eid = [], []
    for lo in range(0, total, SUBLANES):
        cv = ev = None
        for a, st, nb in segs:
            if st + nb <= lo or st >= lo + SUBLANES:
                continue
            if st <= lo and nb > SUBLANES:
                src_s, src_i = s2[lo - st:lo - st + SUBLANES], i2[lo - st:lo - st + SUBLANES]
            else:
                amt = (st - lo) % SUBLANES
                src_s = pltpu.roll(s2[:SUBLANES], amt, axis=0) if amt else s2[:SUBLANES]
                src_i = pltpu.roll(i2[:SUBLANES], amt, axis=0) if amt else i2[:SUBLANES]
            val = s1[a:a + 1] + src_s
            idv = i1[a:a + 1] * PEER_KEYS + src_i
            p0 = max(st - lo, 0)
            cv = val if p0 == 0 else jnp.where(sub >= p0, val, cv)
            ev = idv if p0 == 0 else jnp.where(sub >= p0, idv, ev)
        if total - lo < SUBLANES:
            cv = jnp.where(sub >= total - lo, -jnp.inf, cv)
        cand.append(cv)
        eid.append(ev)
    return jnp.concatenate(cand, axis=0), jnp.concatenate(eid, axis=0)


def _route_kernel(y_ref, wq_ref, sk_ref, idx_ref, gate_ref):
    yb = y_ref[...].astype(BF16)
    tm = yb.shape[0]
    nt = (((1,), (1,)), ((), ()))
    kk = PEER_TOPK
    sub = lax.broadcasted_iota(jnp.int32, (SUBLANES, tm), 0)

    def head(h, carry):
        q = jnp.dot(yb, wq_ref[h], preferred_element_type=F32).astype(BF16)
        top = []
        for part in range(2):
            qp = q[:, part * PEER_HALF:(part + 1) * PEER_HALF]
            sT = lax.dot_general(sk_ref[2 * h + part], qp, nt, preferred_element_type=F32)
            top.append(_top_rows(sT, kk))
        (s1, i1), (s2, i2) = top
        cand, eid = _packed_candidates(s1, i1, s2, i2, sub)
        sc, ids = _top_rows(cand, kk, payload=eid)
        e = jnp.exp(sc - sc[0:1])
        gate = e / jnp.sum(e, axis=0, keepdims=True)
        r0 = pl.multiple_of(h * kk, kk)
        idx_ref[pl.ds(r0, kk), :] = ids
        gate_ref[pl.ds(r0, kk), :] = gate
        return carry

    lax.fori_loop(0, PEER_HEADS, head, 0)


def _route(h1, wq3, sk, *, tm):
    T, D = h1.shape
    return pl.pallas_call(
        _route_kernel,
        grid=(T // tm,),
        in_specs=[pl.BlockSpec((tm, D), lambda i: (i, 0)), _full(wq3.shape), _full(sk.shape)],
        out_specs=[pl.BlockSpec((PEER_SLOTS, tm), lambda i: (0, i)),
                   pl.BlockSpec((PEER_SLOTS, tm), lambda i: (0, i))],
        out_shape=[jax.ShapeDtypeStruct((PEER_SLOTS, T), jnp.int32),
                   jax.ShapeDtypeStruct((PEER_SLOTS, T), F32)],
        compiler_params=_cparams(("arbitrary",)),
        name="peer_route",
    )(h1, wq3, sk)


def _pack_table(t):
    n, dd = t.shape
    tb = t.astype(BF16)
    lo = lax.bitcast_convert_type(tb[:, :dd // 2], jnp.uint16).astype(jnp.uint32)
    hi = lax.bitcast_convert_type(tb[:, dd // 2:], jnp.uint16).astype(jnp.uint32)
    return (lo | (hi << 16)).reshape(n * (dd // (2 * LANES)), LANES)


def _unpack_words(w):
    lo = lax.bitcast_convert_type(w << 16, F32)
    hi = lax.bitcast_convert_type(w & jnp.uint32(0xFFFF0000), F32)
    return lo, hi


def _gather_rows(idx_ref, tab_ref, rs, c, dst):
    irow = idx_ref.at[c]
    for e in range(PEER_SLOTS):
        r0 = pl.multiple_of(irow[e], rs)
        dst[pl.ds(e, rs, stride=CHUNK_STRIDE), :] = tab_ref[pl.ds(r0, rs), :]


def _token_pipeline(tb, gather, fold, ca_ref, cb_ref):
    gather(0, ca_ref)

    def pair(kp, carry):
        c0 = 2 * kp
        gather(c0 + 1, cb_ref)
        carry = fold(c0, ca_ref, carry)
        gather(jnp.minimum(c0 + 2, tb - 1), ca_ref)
        return fold(c0 + 1, cb_ref, carry)

    return pair


def _score_kernel(idx_ref, y_ref, gate_ref, tab_ref, w_ref, ca_ref, cb_ref, *, tb):
    nsl = PEER_SLOTS
    rs = y_ref.shape[1] // 2
    st = CHUNK_STRIDE
    half = rs * LANES
    lane = lax.broadcasted_iota(jnp.int32, (nsl, tb), 1)

    def place(c, tot, act):
        return jnp.where(lane == c, jnp.sum(tot, axis=1, keepdims=True), act)

    def fold(c, src, carry):
        act, pend = carry
        act = place(c - 1, pend, act)
        tot = None
        yv = y_ref[c]
        for s in range(rs):
            lo, hi = _unpack_words(src[s * st:s * st + nsl, :])
            part = lo * yv[s:s + 1] + hi * yv[rs + s:rs + s + 1]
            tot = part if tot is None else tot + part
        return act, tot

    gather = functools.partial(_gather_rows, idx_ref, tab_ref, rs)
    pair = _token_pipeline(tb, gather, fold, ca_ref, cb_ref)
    act, pend = lax.fori_loop(0, tb // 2, pair,
                              (jnp.zeros((nsl, tb), F32), jnp.zeros((nsl, LANES), F32)))
    act = place(tb - 1, pend, act)
    gelu = 0.5 * act * (1.0 + lax.erf(act * (2.0 ** -0.5)))
    w_ref[...] = gate_ref[...] * gelu


def _table_spec(tab):
    return pl.BlockSpec(tab.shape, lambda i: (0, 0), pipeline_mode=pl.Buffered(1))


def _chunk_scratch(rs):
    return pltpu.VMEM((CHUNK_STRIDE * rs, LANES), jnp.uint32)


def _expert_scores(idx, y3, gateT, utab, *, tb):
    T = y3.shape[0]
    kern = functools.partial(_score_kernel, tb=tb)
    return pl.pallas_call(
        kern,
        grid=(T // tb,),
        in_specs=[
            pl.BlockSpec((tb, PEER_SLOTS), lambda i: (i, 0), memory_space=pltpu.SMEM),
            pl.BlockSpec((tb,) + y3.shape[1:], lambda i: (i, 0, 0)),
            pl.BlockSpec((PEER_SLOTS, tb), lambda i: (0, i)),
            _table_spec(utab),
        ],
        out_specs=pl.BlockSpec((PEER_SLOTS, tb), lambda i: (0, i)),
        out_shape=jax.ShapeDtypeStruct((PEER_SLOTS, T), F32),
        scratch_shapes=[_chunk_scratch(y3.shape[1] // 2)] * 2,
        compiler_params=_cparams(("arbitrary",)),
        name="peer_scores",
    )(idx, y3, gateT, utab)


def _mix_kernel(idx_ref, w_ref, tab_ref, f_ref, ca_ref, cb_ref, *, tb):
    nsl = PEER_SLOTS
    rs = f_ref.shape[1] // 2
    st = CHUNK_STRIDE
    half = rs * LANES
    lane = lax.broadcasted_iota(jnp.int32, (nsl, tb), 1)

    def fold(c, src, carry):
        wcol = jnp.sum(jnp.where(lane == c, w_ref[...], 0.0), axis=1, keepdims=True)
        wb = jnp.broadcast_to(wcol, (nsl, LANES))
        los, his = [], []
        for s in range(rs):
            lo, hi = _unpack_words(src[s * st:s * st + nsl, :])
            los.append(jnp.sum(lo * wb, axis=0, keepdims=True))
            his.append(jnp.sum(hi * wb, axis=0, keepdims=True))
        f_ref[c] = jnp.concatenate(los + his, axis=0)
        return carry

    gather = functools.partial(_gather_rows, idx_ref, tab_ref, rs)
    pair = _token_pipeline(tb, gather, fold, ca_ref, cb_ref)
    lax.fori_loop(0, tb // 2, pair, 0)


def _expert_mix(idx, wT, vtab, *, tb, rs):
    T = idx.shape[0]
    rows = 2 * rs
    kern = functools.partial(_mix_kernel, tb=tb)
    return pl.pallas_call(
        kern,
        grid=(T // tb,),
        in_specs=[
            pl.BlockSpec((tb, PEER_SLOTS), lambda i: (i, 0), memory_space=pltpu.SMEM),
            pl.BlockSpec((PEER_SLOTS, tb), lambda i: (0, i)),
            _table_spec(vtab),
        ],
        out_specs=pl.BlockSpec((tb, rows, LANES), lambda i: (i, 0, 0)),
        out_shape=jax.ShapeDtypeStruct((T, rows, LANES), F32),
        scratch_shapes=[_chunk_scratch(rs)] * 2,
        compiler_params=_cparams(("arbitrary",)),
        name="peer_mix",
    )(idx, wT, vtab)


def _ln2_kernel(h_ref, f_ref, g_ref, b_ref, o_ref, *, dn_alpha):
    o_ref[...] = _layer_norm_rows(dn_alpha * h_ref[...] + f_ref[...], g_ref[...], b_ref[...])


def _residual_ln(h1, ffn, g, b, *, tm, dn_alpha):
    T, D = h1.shape
    return pl.pallas_call(
        functools.partial(_ln2_kernel, dn_alpha=dn_alpha),
        grid=(T // tm,),
        in_specs=[pl.BlockSpec((tm, D), lambda i: (i, 0)), pl.BlockSpec((tm, D), lambda i: (i, 0)),
                  _full(g.shape), _full(b.shape)],
        out_specs=pl.BlockSpec((tm, D), lambda i: (i, 0)),
        out_shape=jax.ShapeDtypeStruct((T, D), F32),
        compiler_params=_cparams(("arbitrary",)),
        name="residual_ln2",
    )(h1, ffn, g, b)


def _layer(h2d, B, S, depth, lam_init, w_in, b_gate, lq1, lk1, lq2, lk2, subln_g, w_attn_proj, pool_w,
           pool_scale, w_pool_proj, w_out, ln1_g, ln1_b, peer_wq, peer_subkeys, peer_u, peer_v,
           ln2_g, ln2_b):
    T, D = h2d.shape
    dn_alpha = (2.0 * depth) ** 0.25
    wqk = 2 * N_HEADS * HEAD_DIM
    wv = N_HEADS * V_HEAD_DIM
    wpool = len(POOL_WINDOWS) * POOL_GROUP_DIM
    tm = min(PROJ_ROWS, S)
    ta = min(ATTN_BLOCK, S // 2)
    c0, c1, c2, c3 = wqk, 2 * wqk, 2 * wqk + wv, 2 * wqk + wv + wpool
    wb = w_in.astype(BF16)
    row = lambda a: a.reshape(1, -1).astype(F32)

    qT, k, vTb, p, gates = _project(
        h2d, wb[:, :c0].T, wb[:, c0:c1], wb[:, c1:c2].T, wb[:, c2:c3], wb[:, c3:], row(b_gate),
        tm=tm, ta=ta, scale=HEAD_DIM ** -0.5 * LOG2E)

    slopes = jnp.asarray(np.array([2.0 ** (-8.0 * (h + 1) / N_HEADS) * LOG2E for h in range(N_HEADS)],
                                  dtype=np.float32))
    lamp = jnp.stack([lq1, lk1, lq2, lk2]).astype(F32)
    o = _attention(slopes, lamp, row(subln_g), qT, k, vTb, B=B, S=S, tq=ta, tk=ta, lam_init=lam_init)

    h1 = _merge(o, p, gates, h2d, w_attn_proj.astype(BF16), pool_w.astype(BF16), row(pool_scale),
                w_pool_proj.astype(BF16), w_out.astype(BF16), row(ln1_g), row(ln1_b),
                tm=tm, seq=S, dn_alpha=dn_alpha)

    wq3 = peer_wq.astype(BF16).reshape(D, PEER_HEADS, 2 * PEER_HALF).transpose(1, 0, 2)
    sk = peer_subkeys.astype(BF16).reshape(PEER_HEADS * 2, PEER_KEYS, PEER_HALF)
    idxT, gateT = _route(h1, wq3, sk, tm=min(ROUTE_ROWS, T))

    tb = min(PEER_ROWS, T)
    rs = D // (2 * LANES)
    idx = idxT.T * rs
    wT = _expert_scores(idx, h1.reshape(T, D // LANES, LANES), gateT, _pack_table(peer_u), tb=tb)
    ffn = _expert_mix(idx, wT, _pack_table(peer_v), tb=tb, rs=rs)
    return _residual_ln(h1, ffn.reshape(T, D), row(ln2_g), row(ln2_b), tm=tm, dn_alpha=dn_alpha)


def kernel(x, w_in, b_gate, lambda_q1, lambda_k1, lambda_q2, lambda_k2, subln_g, w_attn_proj, pool_w,
           pool_scale, w_pool_proj, w_out, ln1_g, ln1_b, peer_wq, peer_subkeys, peer_u, peer_v,
           ln2_g, ln2_b):
    B, S, D = x.shape
    depth = w_in.shape[0]
    h = x.reshape(B * S, D)
    for l in range(depth):
        lam_init = 0.8 - 0.6 * math.exp(-0.3 * l)
        h = _layer(h, B, S, depth, lam_init, w_in[l], b_gate[l], lambda_q1[l], lambda_k1[l],
                   lambda_q2[l], lambda_k2[l], subln_g[l], w_attn_proj[l], pool_w[l], pool_scale[l],
                   w_pool_proj[l], w_out[l], ln1_g[l], ln1_b[l], peer_wq[l], peer_subkeys[l],
                   peer_u[l], peer_v[l], ln2_g[l], ln2_b[l])
    return h.reshape(B, S, D)
```

```python
import functools
import math

import jax
import jax.numpy as jnp
import numpy as np
from jax import lax
from jax.experimental import pallas as pl
from jax.experimental.pallas import tpu as pltpu

F32 = jnp.float32
BF16 = jnp.bfloat16

N_HEADS = 8
HEAD_DIM = 64
V_HEAD_DIM = 2 * HEAD_DIM
POOL_WINDOWS = (2, 4, 8, 16)
POOL_GROUP_DIM = 128
POOL_HALO = 16
PEER_HEADS = 8
PEER_KEYS = 128
PEER_HALF = 128
PEER_TOPK = 16
PEER_SLOTS = PEER_HEADS * PEER_TOPK
LN_EPS = 1e-5
LOG2E = math.log2(math.e)

LANES = 128
SUBLANES = 8
VMEM_LIMIT_BYTES = 56 * 1024 * 1024
PROJ_ROWS = 512
ATTN_BLOCK = 256
ROUTE_ROWS = 1024
PEER_ROWS = 128
CHUNK_STRIDE = PEER_SLOTS + 1
DENOM_ROWS = 16
PAIRS_PER_TRIP = 4


def _cparams(sem):
    return pltpu.CompilerParams(dimension_semantics=sem, vmem_limit_bytes=VMEM_LIMIT_BYTES)


def _full(shape):
    n = len(shape)
    return pl.BlockSpec(shape, lambda *_: (0,) * n)


def _proj_kernel(x_ref, wqT_ref, wk_ref, wvT_ref, wp_ref, wg_ref, bg_ref,
                 qT_ref, k_ref, vT_ref, p_ref, gate_ref, *, scale, n_vblk, ta):
    xb = x_ref[...].astype(BF16)
    nt = (((1,), (1,)), ((), ()))
    qT = lax.dot_general(wqT_ref[...], xb, nt, preferred_element_type=F32)
    qT_ref[...] = (qT * scale).astype(BF16)
    k_ref[...] = jnp.dot(xb, wk_ref[...], preferred_element_type=F32).astype(BF16)
    vT = lax.dot_general(wvT_ref[...], xb, nt, preferred_element_type=F32).astype(BF16)
    vd, hr = V_HEAD_DIM, V_HEAD_DIM + DENOM_ROWS
    ones_rows = jnp.where(lax.broadcasted_iota(jnp.int32, (DENOM_ROWS, ta), 0) == 0, 1.0, 0.0).astype(BF16)
    for n in range(n_vblk):
        for h in range(N_HEADS):
            vT_ref[n, h * hr:h * hr + vd, :] = vT[h * vd:(h + 1) * vd, n * ta:(n + 1) * ta]
            vT_ref[n, h * hr + vd:(h + 1) * hr, :] = ones_rows
    p_ref[...] = jnp.dot(xb, wp_ref[...], preferred_element_type=F32)
    g = jnp.dot(xb, wg_ref[...], preferred_element_type=F32) + bg_ref[...]
    gate_ref[...] = jax.nn.sigmoid(g).astype(BF16)


def _project(x2, wqT, wk, wvT, wp, wg, bg, *, tm, ta, scale):
    T, D = x2.shape
    wqk, wv, wpool, wgate = wqT.shape[0], wvT.shape[0], wp.shape[1], wg.shape[1]
    n_vblk = tm // ta
    wvx = N_HEADS * (V_HEAD_DIM + DENOM_ROWS)
    kern = functools.partial(_proj_kernel, scale=scale, n_vblk=n_vblk, ta=ta)
    return pl.pallas_call(
        kern,
        grid=(T // tm,),
        in_specs=[
            pl.BlockSpec((tm, D), lambda i: (i, 0)),
            _full(wqT.shape), _full(wk.shape), _full(wvT.shape), _full(wp.shape), _full(wg.shape),
            _full(bg.shape),
        ],
        out_specs=[
            pl.BlockSpec((wqk, tm), lambda i: (0, i)),
            pl.BlockSpec((tm, wqk), lambda i: (i, 0)),
            pl.BlockSpec((n_vblk, wvx, ta), lambda i: (i, 0, 0)),
            pl.BlockSpec((tm, wpool), lambda i: (i, 0)),
            pl.BlockSpec((tm, wgate), lambda i: (i, 0)),
        ],
        out_shape=[
            jax.ShapeDtypeStruct((wqk, T), BF16),
            jax.ShapeDtypeStruct((T, wqk), BF16),
            jax.ShapeDtypeStruct((T // ta, wvx, ta), BF16),
            jax.ShapeDtypeStruct((T, wpool), F32),
            jax.ShapeDtypeStruct((T, wgate), BF16),
        ],
        compiler_params=_cparams(("arbitrary",)),
        name="input_projection",
    )(x2, wqT, wk, wvT, wp, wg, bg)


def _attn_kernel(slopes_ref, lamp_ref, g_ref, qT_ref, k_ref, vT_ref, o_ref,
                 m_ref, acc_ref, al_ref, base_ref, s0_ref, s1_ref, cm0_ref, cm1_ref, p0_ref, p1_ref,
                 *, tq, tk, lam_init):
    h = pl.program_id(1)
    i = pl.program_id(2)
    slope = slopes_ref[h]
    d = HEAD_DIM
    nrep = 2 * tq // LANES
    qT = qT_ref[...]
    z = jnp.zeros((d, tq), BF16)
    qq = jnp.concatenate([jnp.concatenate([qT[:d], z], axis=0),
                          jnp.concatenate([z, qT[d:]], axis=0)], axis=1)
    m_ref[...] = jnp.full(m_ref.shape, -jnp.inf, F32)
    acc_ref[...] = jnp.zeros(acc_ref.shape, F32)
    al_ref[...] = jnp.ones(al_ref.shape, F32)
    p1_ref[...] = jnp.zeros(p1_ref.shape, BF16)
    base_ref[...] = lax.broadcasted_iota(jnp.int32, (tk, LANES), 0).astype(F32) * slope

    def scores(j, s_ref, cm_ref):
        kb = k_ref[pl.ds(pl.multiple_of(j * tk, tk), tk), :]
        t = jnp.dot(kb, qq, preferred_element_type=F32) + jnp.concatenate([base_ref[...]] * nrep, axis=1)
        s_ref[...] = t
        cm_ref[...] = jnp.max(t, axis=0, keepdims=True)

    def softmax(j, s_ref, cm_ref, p_ref, masked):
        off = (j * tk - i * tq).astype(F32) * slope
        t = s_ref[...]
        if masked:
            key = j * tk + lax.broadcasted_iota(jnp.int32, (tk, 2 * tq), 0)
            col = lax.broadcasted_iota(jnp.int32, (tk, 2 * tq), 1)
            qry = i * tq + jnp.where(col >= tq, col - tq, col)
            t = jnp.where(key > qry, -jnp.inf, t)
            cmax = jnp.max(t, axis=0, keepdims=True)
        else:
            cmax = cm_ref[...]
        m_old = m_ref[...]
        m_new = jnp.maximum(m_old, cmax + off)
        alpha = jnp.exp2(m_old - m_new)
        pr = jnp.exp2(t + (off - m_new))
        p_ref[...] = pr.astype(BF16)
        m_ref[...] = m_new
        return alpha

    def values(j, p_ref):
        vb = vT_ref[jnp.maximum(j, 0)]
        acc_ref[...] = al_ref[...] * acc_ref[...] + jnp.dot(vb, p_ref[...],
                                                             preferred_element_type=F32)

    def half(j, cur, nxt, p_cur, p_prv, masked, prefetch):
        if prefetch:
            scores(j + 1, *nxt)
        alpha = softmax(j, *cur, p_cur, masked)
        values(j - 1, p_prv)
        al_ref[...] = alpha

    def pair(t, masked, last):
        j0 = 2 * t
        half(j0, buf0, buf1, p0_ref, p1_ref, masked, True)
        half(j0 + 1, buf1, buf0, p1_ref, p0_ref, masked, not last)

    buf0, buf1 = (s0_ref, cm0_ref), (s1_ref, cm1_ref)
    scores(0, *buf0)

    def body(t, carry):
        for u in range(PAIRS_PER_TRIP):
            pair(PAIRS_PER_TRIP * t + u, False, False)
        return carry

    npairs = (i * tq) // (2 * tk)
    ntrips = npairs // PAIRS_PER_TRIP
    lax.fori_loop(0, ntrips, body, 0)
    done = ntrips * PAIRS_PER_TRIP
    step = PAIRS_PER_TRIP // 2
    while step:
        take = ((npairs - done) & step) != 0

        @pl.when(take)
        def _(done=done, step=step):
            for u in range(step):
                pair(done + u, False, False)

        done = done + jnp.where(take, step, 0)
        step //= 2

    pair(npairs, True, True)
    values(2 * npairs + 1, p1_ref)

    lamp = lamp_ref[...]
    lam = (jnp.exp(jnp.sum(lamp[0:1] * lamp[1:2], axis=1, keepdims=True))
           - jnp.exp(jnp.sum(lamp[2:3] * lamp[3:4], axis=1, keepdims=True)) + lam_init)
    acc = acc_ref[0:2 * d, :]
    l = acc_ref[2 * d:2 * d + 1, :]
    oT = acc[:, :tq] / l[:, :tq] - lam * (acc[:, tq:] / l[:, tq:])
    ms = jnp.mean(oT * oT, axis=0, keepdims=True)
    oT = oT * lax.rsqrt(ms + LN_EPS)
    o = oT.T * g_ref[...] * (1.0 - lam_init)
    o_ref[...] = o.astype(BF16)


def _attention(slopes, lamp, subln_g, qT, k, vTb, *, B, S, tq, tk, lam_init):
    T = B * S
    nq = S // tq
    nk = S // tk
    assert tq in (tk, 2 * tk) and nk % 2 == 0
    vd = V_HEAD_DIM
    kern = functools.partial(_attn_kernel, tq=tq, tk=tk, lam_init=lam_init)
    return pl.pallas_call(
        kern,
        grid=(B, N_HEADS, nq),
        in_specs=[
            pl.BlockSpec(memory_space=pltpu.SMEM),
            _full(lamp.shape),
            _full(subln_g.shape),
            pl.BlockSpec((vd, tq), lambda b, h, i: (h, b * nq + i)),
            pl.BlockSpec((S, vd), lambda b, h, i: (b, h)),
            pl.BlockSpec((nk, vd + DENOM_ROWS, tk), lambda b, h, i: (b, h, 0)),
        ],
        out_specs=pl.BlockSpec((tq, vd), lambda b, h, i: (b * nq + i, h)),
        out_shape=jax.ShapeDtypeStruct((T, N_HEADS * vd), BF16),
        scratch_shapes=[
            pltpu.VMEM((1, 2 * tq), F32),
            pltpu.VMEM((vd + DENOM_ROWS, 2 * tq), F32),
            pltpu.VMEM((1, 2 * tq), F32),
            pltpu.VMEM((tk, LANES), F32),
            pltpu.VMEM((tk, 2 * tq), F32),
            pltpu.VMEM((tk, 2 * tq), F32),
            pltpu.VMEM((1, 2 * tq), F32),
            pltpu.VMEM((1, 2 * tq), F32),
            pltpu.VMEM((tk, 2 * tq), BF16),
            pltpu.VMEM((tk, 2 * tq), BF16),
        ],
        compiler_params=_cparams(("arbitrary", "arbitrary", "arbitrary")),
        name="diff_attention",
    )(slopes, lamp, subln_g, qT, k, vTb)


def _layer_norm_rows(r, g, b):
    mu = jnp.mean(r, axis=-1, keepdims=True)
    dlt = r - mu
    var = jnp.mean(dlt * dlt, axis=-1, keepdims=True)
    return dlt * lax.rsqrt(var + LN_EPS) * g + b


def _merge_kernel(o_ref, p_ref, ph_ref, gate_ref, x_ref, wap_ref, pw_ref, ps_ref, wpp_ref,
                  wout_ref, g1_ref, b1_ref, h1_ref, ext_ref, *, tm, seq, dn_alpha):
    i = pl.program_id(0)
    t0 = (i * tm) % seq
    hl = POOL_HALO
    ext_ref[0:hl, :] = jnp.where(t0 == 0, 0.0, ph_ref[...])
    ext_ref[hl:hl + tm, :] = p_ref[...]
    pos = (t0 + lax.broadcasted_iota(jnp.int32, (tm, POOL_GROUP_DIM), 0)).astype(F32)
    pm = []
    for gi, w in enumerate(POOL_WINDOWS):
        sl = slice(gi * POOL_GROUP_DIM, (gi + 1) * POOL_GROUP_DIM)
        cur = ext_ref[hl:hl + tm, sl]
        win = cur
        for back in range(1, w):
            win = win + ext_ref[hl - back:hl - back + tm, sl]
        cnt = jnp.minimum(float(w), pos + 1.0)
        pooled = win / cnt - cur
        pm.append(jnp.dot(pooled.astype(BF16), pw_ref[gi], preferred_element_type=F32))
    pm = jnp.concatenate(pm, axis=1) * ps_ref[...]
    pool_branch = jnp.dot(pm.astype(BF16), wpp_ref[...], preferred_element_type=F32)
    attn_branch = jnp.dot(o_ref[...], wap_ref[...], preferred_element_type=F32)
    dm = attn_branch.shape[1]
    gates = gate_ref[...].astype(F32)
    merged = gates[:, :dm] * attn_branch + gates[:, dm:] * pool_branch
    mix = jnp.dot(merged.astype(BF16), wout_ref[...], preferred_element_type=F32)
    h1_ref[...] = _layer_norm_rows(dn_alpha * x_ref[...] + mix, g1_ref[...], b1_ref[...])


def _merge(o, p, gates, x2, wap, pw, ps, wpp, wout, g1, b1, *, tm, seq, dn_alpha):
    T, D = x2.shape
    pwid = p.shape[1]
    hb = tm // POOL_HALO
    kern = functools.partial(_merge_kernel, tm=tm, seq=seq, dn_alpha=dn_alpha)
    return pl.pallas_call(
        kern,
        grid=(T // tm,),
        in_specs=[
            pl.BlockSpec((tm, o.shape[1]), lambda i: (i, 0)),
            pl.BlockSpec((tm, pwid), lambda i: (i, 0)),
            pl.BlockSpec((POOL_HALO, pwid), lambda i: (jnp.maximum(i * hb - 1, 0), 0)),
            pl.BlockSpec((tm, gates.shape[1]), lambda i: (i, 0)),
            pl.BlockSpec((tm, D), lambda i: (i, 0)),
            _full(wap.shape), _full(pw.shape), _full(ps.shape), _full(wpp.shape), _full(wout.shape),
            _full(g1.shape), _full(b1.shape),
        ],
        out_specs=pl.BlockSpec((tm, D), lambda i: (i, 0)),
        out_shape=jax.ShapeDtypeStruct((T, D), F32),
        scratch_shapes=[pltpu.VMEM((POOL_HALO + tm, pwid), F32)],
        compiler_params=_cparams(("arbitrary",)),
        name="merge_ln1",
    )(o, p, p, gates, x2, wap, pw, ps, wpp, wout, g1, b1)


def _top_rows(s, k, payload=None):
    n = s.shape[0]
    row = lax.broadcasted_iota(jnp.int32, s.shape, 0)
    vals, picks = [], []
    for _ in range(k):
        m = jnp.max(s, axis=0, keepdims=True)
        am = jnp.min(jnp.where(s == m, row, n), axis=0, keepdims=True)
        hit = row == am
        vals.append(m)
        if payload is None:
            picks.append(am)
        else:
            picks.append(jnp.sum(jnp.where(hit, payload, 0), axis=0, keepdims=True))
        s = jnp.where(hit, -jnp.inf, s)
    return jnp.concatenate(vals, axis=0), jnp.concatenate(picks, axis=0)


def _packed_candidates(s1, i1, s2, i2, sub):
    kk = s1.shape[0]
    segs, start = [], 0
    for a in range(kk):
        segs.append((a, start, kk // (a + 1)))
        start += kk // (a + 1)
    total = start
    cand, eid = [], []
    for lo in range(0, total, SUBLANES):
        cv = ev = None
        for a, st, nb in segs:
            if st + nb <= lo or st >= lo + SUBLANES:
                continue
            if st <= lo and nb > SUBLANES:
                src_s, src_i = s2[lo - st:lo - st + SUBLANES], i2[lo - st:lo - st + SUBLANES]
            else:
                amt = (st - lo) % SUBLANES
                src_s = pltpu.roll(s2[:SUBLANES], amt, axis=0) if amt else s2[:SUBLANES]
                src_i = pltpu.roll(i2[:SUBLANES], amt, axis=0) if amt else i2[:SUBLANES]
            val = s1[a:a + 1] + src_s
            idv = i1[a:a + 1] * PEER_KEYS + src_i
            p0 = max(st - lo, 0)
            cv = val if p0 == 0 else jnp.where(sub >= p0, val, cv)
            ev = idv if p0 == 0 else jnp.where(sub >= p0, idv, ev)
        if total - lo < SUBLANES:
            cv = jnp.where(sub >= total - lo, -jnp.inf, cv)
        cand.append(cv)
        eid.append(ev)
    return jnp.concatenate(cand, axis=0), jnp.concatenate(eid, axis=0)


def _route_kernel(y_ref, wq_ref, sk_ref, idx_ref, gate_ref):
    yb = y_ref[...].astype(BF16)
    tm = yb.shape[0]
    nt = (((1,), (1,)), ((), ()))
    kk = PEER_TOPK
    sub = lax.broadcasted_iota(jnp.int32, (SUBLANES, tm), 0)

    def head(h, carry):
        q = jnp.dot(yb, wq_ref[h], preferred_element_type=F32).astype(BF16)
        top = []
        for part in range(2):
            qp = q[:, part * PEER_HALF:(part + 1) * PEER_HALF]
            sT = lax.dot_general(sk_ref[2 * h + part], qp, nt, preferred_element_type=F32)
            top.append(_top_rows(sT, kk))
        (s1, i1), (s2, i2) = top
        cand, eid = _packed_candidates(s1, i1, s2, i2, sub)
        sc, ids = _top_rows(cand, kk, payload=eid)
        e = jnp.exp(sc - sc[0:1])
        gate = e / jnp.sum(e, axis=0, keepdims=True)
        r0 = pl.multiple_of(h * kk, kk)
        idx_ref[pl.ds(r0, kk), :] = ids
        gate_ref[pl.ds(r0, kk), :] = gate
        return carry

    lax.fori_loop(0, PEER_HEADS, head, 0)


def _route(h1, wq3, sk, *, tm):
    T, D = h1.shape
    return pl.pallas_call(
        _route_kernel,
        grid=(T // tm,),
        in_specs=[pl.BlockSpec((tm, D), lambda i: (i, 0)), _full(wq3.shape), _full(sk.shape)],
        out_specs=[pl.BlockSpec((PEER_SLOTS, tm), lambda i: (0, i)),
                   pl.BlockSpec((PEER_SLOTS, tm), lambda i: (0, i))],
        out_shape=[jax.ShapeDtypeStruct((PEER_SLOTS, T), jnp.int32),
                   jax.ShapeDtypeStruct((PEER_SLOTS, T), F32)],
        compiler_params=_cparams(("arbitrary",)),
        name="peer_route",
    )(h1, wq3, sk)


def _pack_table(t):
    n, dd = t.shape
    bits = lax.bitcast_convert_type(t, jnp.uint32)
    b16 = (bits + jnp.uint32(0x7FFF) + ((bits >> 16) & jnp.uint32(1))) >> 16
    return (b16[:, :dd // 2] | (b16[:, dd // 2:] << 16)).reshape(n * (dd // (2 * LANES)), LANES)


def _unpack_words(w):
    lo = lax.bitcast_convert_type(w << 16, F32)
    hi = lax.bitcast_convert_type(w & jnp.uint32(0xFFFF0000), F32)
    return lo, hi


def _gather_rows(idx_ref, tab_ref, rs, c, dst):
    irow = idx_ref.at[c]
    for e in range(PEER_SLOTS):
        r0 = pl.multiple_of(irow[e], rs)
        dst[pl.ds(e, rs, stride=CHUNK_STRIDE), :] = tab_ref[pl.ds(r0, rs), :]


def _token_pipeline(tb, gather, fold, ca_ref, cb_ref):
    gather(0, ca_ref)

    def pair(kp, carry):
        c0 = 2 * kp
        gather(c0 + 1, cb_ref)
        carry = fold(c0, ca_ref, carry)
        gather(jnp.minimum(c0 + 2, tb - 1), ca_ref)
        return fold(c0 + 1, cb_ref, carry)

    return pair


def _score_kernel(idx_ref, y_ref, gate_ref, tab_ref, w_ref, ca_ref, cb_ref, *, tb):
    nsl = PEER_SLOTS
    rs = y_ref.shape[1] // 2
    st = CHUNK_STRIDE
    half = rs * LANES
    lane = lax.broadcasted_iota(jnp.int32, (nsl, tb), 1)

    def place(c, tot, act):
        return jnp.where(lane == c, jnp.sum(tot, axis=1, keepdims=True), act)

    def fold(c, src, carry):
        act, pend = carry
        act = place(c - 1, pend, act)
        tot = None
        yv = y_ref[c]
        for s in range(rs):
            lo, hi = _unpack_words(src[s * st:s * st + nsl, :])
            part = lo * yv[s:s + 1] + hi * yv[rs + s:rs + s + 1]
            tot = part if tot is None else tot + part
        return act, tot

    gather = functools.partial(_gather_rows, idx_ref, tab_ref, rs)
    pair = _token_pipeline(tb, gather, fold, ca_ref, cb_ref)
    act, pend = lax.fori_loop(0, tb // 2, pair,
                              (jnp.zeros((nsl, tb), F32), jnp.zeros((nsl, LANES), F32)))
    act = place(tb - 1, pend, act)
    gelu = 0.5 * act * (1.0 + lax.erf(act * (2.0 ** -0.5)))
    w_ref[...] = gate_ref[...] * gelu


def _table_spec(tab):
    return pl.BlockSpec(tab.shape, lambda i: (0, 0), pipeline_mode=pl.Buffered(1))


def _chunk_scratch(rs):
    return pltpu.VMEM((CHUNK_STRIDE * rs, LANES), jnp.uint32)


def _expert_scores(idx, y3, gateT, utab, *, tb):
    T = y3.shape[0]
    kern = functools.partial(_score_kernel, tb=tb)
    return pl.pallas_call(
        kern,
        grid=(T // tb,),
        in_specs=[
            pl.BlockSpec((tb, PEER_SLOTS), lambda i: (i, 0), memory_space=pltpu.SMEM),
            pl.BlockSpec((tb,) + y3.shape[1:], lambda i: (i, 0, 0)),
            pl.BlockSpec((PEER_SLOTS, tb), lambda i: (0, i)),
            _table_spec(utab),
        ],
        out_specs=pl.BlockSpec((PEER_SLOTS, tb), lambda i: (0, i)),
        out_shape=jax.ShapeDtypeStruct((PEER_SLOTS, T), F32),
        scratch_shapes=[_chunk_scratch(y3.shape[1] // 2)] * 2,
        compiler_params=_cparams(("arbitrary",)),
        name="peer_scores",
    )(idx, y3, gateT, utab)


def _mix_kernel(idx_ref, w_ref, tab_ref, f_ref, ca_ref, cb_ref, *, tb):
    nsl = PEER_SLOTS
    rs = f_ref.shape[1] // 2
    st = CHUNK_STRIDE
    half = rs * LANES
    lane = lax.broadcasted_iota(jnp.int32, (nsl, tb), 1)

    def fold(c, src, carry):
        wcol = jnp.sum(jnp.where(lane == c, w_ref[...], 0.0), axis=1, keepdims=True)
        wb = jnp.broadcast_to(wcol, (nsl, LANES))
        los, his = [], []
        for s in range(rs):
            lo, hi = _unpack_words(src[s * st:s * st + nsl, :])
            los.append(jnp.sum(lo * wb, axis=0, keepdims=True))
            his.append(jnp.sum(hi * wb, axis=0, keepdims=True))
        f_ref[c] = jnp.concatenate(los + his, axis=0)
        return carry

    gather = functools.partial(_gather_rows, idx_ref, tab_ref, rs)
    pair = _token_pipeline(tb, gather, fold, ca_ref, cb_ref)
    lax.fori_loop(0, tb // 2, pair, 0)


def _expert_mix(idx, wT, vtab, *, tb, rs):
    T = idx.shape[0]
    rows = 2 * rs
    kern = functools.partial(_mix_kernel, tb=tb)
    return pl.pallas_call(
        kern,
        grid=(T // tb,),
        in_specs=[
            pl.BlockSpec((tb, PEER_SLOTS), lambda i: (i, 0), memory_space=pltpu.SMEM),
            pl.BlockSpec((PEER_SLOTS, tb), lambda i: (0, i)),
            _table_spec(vtab),
        ],
        out_specs=pl.BlockSpec((tb, rows, LANES), lambda i: (i, 0, 0)),
        out_shape=jax.ShapeDtypeStruct((T, rows, LANES), F32),
        scratch_shapes=[_chunk_scratch(rs)] * 2,
        compiler_params=_cparams(("arbitrary",)),
        name="peer_mix",
    )(idx, wT, vtab)


def _ln2_kernel(h_ref, f_ref, g_ref, b_ref, o_ref, *, dn_alpha):
    o_ref[...] = _layer_norm_rows(dn_alpha * h_ref[...] + f_ref[...], g_ref[...], b_ref[...])


def _residual_ln(h1, ffn, g, b, *, tm, dn_alpha):
    T, D = h1.shape
    return pl.pallas_call(
        functools.partial(_ln2_kernel, dn_alpha=dn_alpha),
        grid=(T // tm,),
        in_specs=[pl.BlockSpec((tm, D), lambda i: (i, 0)), pl.BlockSpec((tm, D), lambda i: (i, 0)),
                  _full(g.shape), _full(b.shape)],
        out_specs=pl.BlockSpec((tm, D), lambda i: (i, 0)),
        out_shape=jax.ShapeDtypeStruct((T, D), F32),
        compiler_params=_cparams(("arbitrary",)),
        name="residual_ln2",
    )(h1, ffn, g, b)


def _layer(h2d, B, S, depth, lam_init, w_in, b_gate, lq1, lk1, lq2, lk2, subln_g, w_attn_proj, pool_w,
           pool_scale, w_pool_proj, w_out, ln1_g, ln1_b, peer_wq, peer_subkeys, peer_u, peer_v,
           ln2_g, ln2_b):
    T, D = h2d.shape
    dn_alpha = (2.0 * depth) ** 0.25
    wqk = 2 * N_HEADS * HEAD_DIM
    wv = N_HEADS * V_HEAD_DIM
    wpool = len(POOL_WINDOWS) * POOL_GROUP_DIM
    tm = min(PROJ_ROWS, S)
    ta = min(ATTN_BLOCK, S // 2)
    c0, c1, c2, c3 = wqk, 2 * wqk, 2 * wqk + wv, 2 * wqk + wv + wpool
    wb = w_in.astype(BF16)
    row = lambda a: a.reshape(1, -1).astype(F32)

    qT, k, vTb, p, gates = _project(
        h2d, wb[:, :c0].T, wb[:, c0:c1], wb[:, c1:c2].T, wb[:, c2:c3], wb[:, c3:], row(b_gate),
        tm=tm, ta=ta, scale=HEAD_DIM ** -0.5 * LOG2E)

    slopes = jnp.asarray(np.array([2.0 ** (-8.0 * (h + 1) / N_HEADS) * LOG2E for h in range(N_HEADS)],
                                  dtype=np.float32))
    lamp = jnp.stack([lq1, lk1, lq2, lk2]).astype(F32)
    o = _attention(slopes, lamp, row(subln_g), qT, k, vTb, B=B, S=S, tq=ta, tk=ta, lam_init=lam_init)

    h1 = _merge(o, p, gates, h2d, w_attn_proj.astype(BF16), pool_w.astype(BF16), row(pool_scale),
                w_pool_proj.astype(BF16), w_out.astype(BF16), row(ln1_g), row(ln1_b),
                tm=tm, seq=S, dn_alpha=dn_alpha)

    wq3 = peer_wq.astype(BF16).reshape(D, PEER_HEADS, 2 * PEER_HALF).transpose(1, 0, 2)
    sk = peer_subkeys.astype(BF16).reshape(PEER_HEADS * 2, PEER_KEYS, PEER_HALF)
    idxT, gateT = _route(h1, wq3, sk, tm=min(ROUTE_ROWS, T))

    tb = min(PEER_ROWS, T)
    rs = D // (2 * LANES)
    idx = idxT.T * rs
    wT = _expert_scores(idx, h1.reshape(T, D // LANES, LANES), gateT, _pack_table(peer_u), tb=tb)
    ffn = _expert_mix(idx, wT, _pack_table(peer_v), tb=tb, rs=rs)
    return _residual_ln(h1, ffn.reshape(T, D), row(ln2_g), row(ln2_b), tm=tm, dn_alpha=dn_alpha)


def kernel(x, w_in, b_gate, lambda_q1, lambda_k1, lambda_q2, lambda_k2, subln_g, w_attn_proj, pool_w,
           pool_scale, w_pool_proj, w_out, ln1_g, ln1_b, peer_wq, peer_subkeys, peer_u, peer_v,
           ln2_g, ln2_b):
    B, S, D = x.shape
    depth = w_in.shape[0]
    h = x.reshape(B * S, D)
    for l in range(depth):
        lam_init = 0.8 - 0.6 * math.exp(-0.3 * l)
        h = _layer(h, B, S, depth, lam_init, w_in[l], b_gate[l], lambda_q1[l], lambda_k1[l],
                   lambda_q2[l], lambda_k2[l], subln_g[l], w_attn_proj[l], pool_w[l], pool_scale[l],
                   w_pool_proj[l], w_out[l], ln1_g[l], ln1_b[l], peer_wq[l], peer_subkeys[l],
                   peer_u[l], peer_v[l], ln2_g[l], ln2_b[l])
    return h.reshape(B, S, D)
```

```python
import functools
import math

import jax
import jax.numpy as jnp
import numpy as np
from jax import lax
from jax.experimental import pallas as pl
from jax.experimental.pallas import tpu as pltpu

F32 = jnp.float32
BF16 = jnp.bfloat16

N_HEADS = 8
HEAD_DIM = 64
V_HEAD_DIM = 2 * HEAD_DIM
POOL_WINDOWS = (2, 4, 8, 16)
POOL_GROUP_DIM = 128
POOL_HALO = 16
PEER_HEADS = 8
PEER_KEYS = 128
PEER_HALF = 128
PEER_TOPK = 16
PEER_SLOTS = PEER_HEADS * PEER_TOPK
LN_EPS = 1e-5
LOG2E = math.log2(math.e)

LANES = 128
SUBLANES = 8
VMEM_LIMIT_BYTES = 56 * 1024 * 1024
PROJ_ROWS = 512
ATTN_BLOCK = 256
ROUTE_ROWS = 2048
PEER_ROWS = 128
CHUNK_STRIDE = PEER_SLOTS + 1
DENOM_ROWS = 16
PAIRS_PER_TRIP = 4


def _cparams(sem):
    return pltpu.CompilerParams(dimension_semantics=sem, vmem_limit_bytes=VMEM_LIMIT_BYTES)


def _full(shape):
    n = len(shape)
    return pl.BlockSpec(shape, lambda *_: (0,) * n)


def _proj_kernel(x_ref, wqT_ref, wk_ref, wvT_ref, wp_ref, wg_ref, bg_ref,
                 qT_ref, k_ref, vT_ref, p_ref, gate_ref, *, scale, n_vblk, ta):
    xb = x_ref[...].astype(BF16)
    nt = (((1,), (1,)), ((), ()))
    qT = lax.dot_general(wqT_ref[...], xb, nt, preferred_element_type=F32)
    qT_ref[...] = (qT * scale).astype(BF16)
    k_ref[...] = jnp.dot(xb, wk_ref[...], preferred_element_type=F32).astype(BF16)
    vT = lax.dot_general(wvT_ref[...], xb, nt, preferred_element_type=F32).astype(BF16)
    vd, hr = V_HEAD_DIM, V_HEAD_DIM + DENOM_ROWS
    ones_rows = jnp.where(lax.broadcasted_iota(jnp.int32, (DENOM_ROWS, ta), 0) == 0, 1.0, 0.0).astype(BF16)
    for n in range(n_vblk):
        for h in range(N_HEADS):
            vT_ref[n, h * hr:h * hr + vd, :] = vT[h * vd:(h + 1) * vd, n * ta:(n + 1) * ta]
            vT_ref[n, h * hr + vd:(h + 1) * hr, :] = ones_rows
    p_ref[...] = jnp.dot(xb, wp_ref[...], preferred_element_type=F32)
    g = jnp.dot(xb, wg_ref[...], preferred_element_type=F32) + bg_ref[...]
    gate_ref[...] = jax.nn.sigmoid(g).astype(BF16)


def _project(x2, wqT, wk, wvT, wp, wg, bg, *, tm, ta, scale):
    T, D = x2.shape
    wqk, wv, wpool, wgate = wqT.shape[0], wvT.shape[0], wp.shape[1], wg.shape[1]
    n_vblk = tm // ta
    wvx = N_HEADS * (V_HEAD_DIM + DENOM_ROWS)
    kern = functools.partial(_proj_kernel, scale=scale, n_vblk=n_vblk, ta=ta)
    return pl.pallas_call(
        kern,
        grid=(T // tm,),
        in_specs=[
            pl.BlockSpec((tm, D), lambda i: (i, 0)),
            _full(wqT.shape), _full(wk.shape), _full(wvT.shape), _full(wp.shape), _full(wg.shape),
            _full(bg.shape),
        ],
        out_specs=[
            pl.BlockSpec((wqk, tm), lambda i: (0, i)),
            pl.BlockSpec((tm, wqk), lambda i: (i, 0)),
            pl.BlockSpec((n_vblk, wvx, ta), lambda i: (i, 0, 0)),
            pl.BlockSpec((tm, wpool), lambda i: (i, 0)),
            pl.BlockSpec((tm, wgate), lambda i: (i, 0)),
        ],
        out_shape=[
            jax.ShapeDtypeStruct((wqk, T), BF16),
            jax.ShapeDtypeStruct((T, wqk), BF16),
            jax.ShapeDtypeStruct((T // ta, wvx, ta), BF16),
            jax.ShapeDtypeStruct((T, wpool), F32),
            jax.ShapeDtypeStruct((T, wgate), BF16),
        ],
        compiler_params=_cparams(("arbitrary",)),
        name="input_projection",
    )(x2, wqT, wk, wvT, wp, wg, bg)


def _attn_kernel(slopes_ref, lamp_ref, g_ref, qT_ref, k_ref, vT_ref, o_ref,
                 m_ref, acc_ref, al_ref, base_ref, s0_ref, s1_ref, cm0_ref, cm1_ref, p0_ref, p1_ref,
                 *, tq, tk, lam_init):
    h = pl.program_id(1)
    i = pl.program_id(2)
    slope = slopes_ref[h]
    d = HEAD_DIM
    nrep = 2 * tq // LANES
    qT = qT_ref[...]
    z = jnp.zeros((d, tq), BF16)
    qq = jnp.concatenate([jnp.concatenate([qT[:d], z], axis=0),
                          jnp.concatenate([z, qT[d:]], axis=0)], axis=1)
    m_ref[...] = jnp.full(m_ref.shape, -jnp.inf, F32)
    acc_ref[...] = jnp.zeros(acc_ref.shape, F32)
    al_ref[...] = jnp.ones(al_ref.shape, F32)
    p1_ref[...] = jnp.zeros(p1_ref.shape, BF16)
    base_ref[...] = lax.broadcasted_iota(jnp.int32, (tk, LANES), 0).astype(F32) * slope

    def scores(j, s_ref, cm_ref):
        kb = k_ref[pl.ds(pl.multiple_of(j * tk, tk), tk), :]
        t = jnp.dot(kb, qq, preferred_element_type=F32) + jnp.concatenate([base_ref[...]] * nrep, axis=1)
        s_ref[...] = t
        cm_ref[...] = jnp.max(t, axis=0, keepdims=True)

    def softmax(j, s_ref, cm_ref, p_ref, masked):
        off = (j * tk - i * tq).astype(F32) * slope
        t = s_ref[...]
        if masked:
            key = j * tk + lax.broadcasted_iota(jnp.int32, (tk, 2 * tq), 0)
            col = lax.broadcasted_iota(jnp.int32, (tk, 2 * tq), 1)
            qry = i * tq + jnp.where(col >= tq, col - tq, col)
            t = jnp.where(key > qry, -jnp.inf, t)
            cmax = jnp.max(t, axis=0, keepdims=True)
        else:
            cmax = cm_ref[...]
        m_old = m_ref[...]
        m_new = jnp.maximum(m_old, cmax + off)
        alpha = jnp.exp2(m_old - m_new)
        pr = jnp.exp2(t + (off - m_new))
        p_ref[...] = pr.astype(BF16)
        m_ref[...] = m_new
        return alpha

    def values(j, p_ref):
        vb = vT_ref[jnp.maximum(j, 0)]
        acc_ref[...] = al_ref[...] * acc_ref[...] + jnp.dot(vb, p_ref[...],
                                                             preferred_element_type=F32)

    def half(j, cur, nxt, p_cur, p_prv, masked, prefetch):
        if prefetch:
            scores(j + 1, *nxt)
        alpha = softmax(j, *cur, p_cur, masked)
        values(j - 1, p_prv)
        al_ref[...] = alpha

    def pair(t, masked, last):
        j0 = 2 * t
        half(j0, buf0, buf1, p0_ref, p1_ref, masked, True)
        half(j0 + 1, buf1, buf0, p1_ref, p0_ref, masked, not last)

    buf0, buf1 = (s0_ref, cm0_ref), (s1_ref, cm1_ref)
    scores(0, *buf0)

    def body(t, carry):
        for u in range(PAIRS_PER_TRIP):
            pair(PAIRS_PER_TRIP * t + u, False, False)
        return carry

    npairs = (i * tq) // (2 * tk)
    ntrips = npairs // PAIRS_PER_TRIP
    lax.fori_loop(0, ntrips, body, 0)
    done = ntrips * PAIRS_PER_TRIP
    step = PAIRS_PER_TRIP // 2
    while step:
        take = ((npairs - done) & step) != 0

        @pl.when(take)
        def _(done=done, step=step):
            for u in range(step):
                pair(done + u, False, False)

        done = done + jnp.where(take, step, 0)
        step //= 2

    pair(npairs, True, True)
    values(2 * npairs + 1, p1_ref)

    lamp = lamp_ref[...]
    lam = (jnp.exp(jnp.sum(lamp[0:1] * lamp[1:2], axis=1, keepdims=True))
           - jnp.exp(jnp.sum(lamp[2:3] * lamp[3:4], axis=1, keepdims=True)) + lam_init)
    acc = acc_ref[0:2 * d, :]
    l = acc_ref[2 * d:2 * d + 1, :]
    oT = acc[:, :tq] / l[:, :tq] - lam * (acc[:, tq:] / l[:, tq:])
    ms = jnp.mean(oT * oT, axis=0, keepdims=True)
    oT = oT * lax.rsqrt(ms + LN_EPS)
    o = oT.T * g_ref[...] * (1.0 - lam_init)
    o_ref[...] = o.astype(BF16)


def _attention(slopes, lamp, subln_g, qT, k, vTb, *, B, S, tq, tk, lam_init):
    T = B * S
    nq = S // tq
    nk = S // tk
    assert tq in (tk, 2 * tk) and nk % 2 == 0
    vd = V_HEAD_DIM
    kern = functools.partial(_attn_kernel, tq=tq, tk=tk, lam_init=lam_init)
    return pl.pallas_call(
        kern,
        grid=(B, N_HEADS, nq),
        in_specs=[
            pl.BlockSpec(memory_space=pltpu.SMEM),
            _full(lamp.shape),
            _full(subln_g.shape),
            pl.BlockSpec((vd, tq), lambda b, h, i: (h, b * nq + i)),
            pl.BlockSpec((S, vd), lambda b, h, i: (b, h)),
            pl.BlockSpec((nk, vd + DENOM_ROWS, tk), lambda b, h, i: (b, h, 0)),
        ],
        out_specs=pl.BlockSpec((tq, vd), lambda b, h, i: (b * nq + i, h)),
        out_shape=jax.ShapeDtypeStruct((T, N_HEADS * vd), BF16),
        scratch_shapes=[
            pltpu.VMEM((1, 2 * tq), F32),
            pltpu.VMEM((vd + DENOM_ROWS, 2 * tq), F32),
            pltpu.VMEM((1, 2 * tq), F32),
            pltpu.VMEM((tk, LANES), F32),
            pltpu.VMEM((tk, 2 * tq), F32),
            pltpu.VMEM((tk, 2 * tq), F32),
            pltpu.VMEM((1, 2 * tq), F32),
            pltpu.VMEM((1, 2 * tq), F32),
            pltpu.VMEM((tk, 2 * tq), BF16),
            pltpu.VMEM((tk, 2 * tq), BF16),
        ],
        compiler_params=_cparams(("arbitrary", "arbitrary", "arbitrary")),
        name="diff_attention",
    )(slopes, lamp, subln_g, qT, k, vTb)


def _layer_norm_rows(r, g, b):
    mu = jnp.mean(r, axis=-1, keepdims=True)
    dlt = r - mu
    var = jnp.mean(dlt * dlt, axis=-1, keepdims=True)
    return dlt * lax.rsqrt(var + LN_EPS) * g + b


def _merge_kernel(o_ref, p_ref, ph_ref, gate_ref, x_ref, wap_ref, pw_ref, ps_ref, wpp_ref,
                  wout_ref, g1_ref, b1_ref, h1_ref, ext_ref, *, tm, seq, dn_alpha):
    i = pl.program_id(0)
    t0 = (i * tm) % seq
    hl = POOL_HALO
    ext_ref[0:hl, :] = jnp.where(t0 == 0, 0.0, ph_ref[...])
    ext_ref[hl:hl + tm, :] = p_ref[...]
    pos = (t0 + lax.broadcasted_iota(jnp.int32, (tm, POOL_GROUP_DIM), 0)).astype(F32)
    pm = []
    for gi, w in enumerate(POOL_WINDOWS):
        sl = slice(gi * POOL_GROUP_DIM, (gi + 1) * POOL_GROUP_DIM)
        cur = ext_ref[hl:hl + tm, sl]
        win = cur
        for back in range(1, w):
            win = win + ext_ref[hl - back:hl - back + tm, sl]
        cnt = jnp.minimum(float(w), pos + 1.0)
        pooled = win / cnt - cur
        pm.append(jnp.dot(pooled.astype(BF16), pw_ref[gi], preferred_element_type=F32))
    pm = jnp.concatenate(pm, axis=1) * ps_ref[...]
    pool_branch = jnp.dot(pm.astype(BF16), wpp_ref[...], preferred_element_type=F32)
    attn_branch = jnp.dot(o_ref[...], wap_ref[...], preferred_element_type=F32)
    dm = attn_branch.shape[1]
    gates = gate_ref[...].astype(F32)
    merged = gates[:, :dm] * attn_branch + gates[:, dm:] * pool_branch
    mix = jnp.dot(merged.astype(BF16), wout_ref[...], preferred_element_type=F32)
    h1_ref[...] = _layer_norm_rows(dn_alpha * x_ref[...] + mix, g1_ref[...], b1_ref[...])


def _merge(o, p, gates, x2, wap, pw, ps, wpp, wout, g1, b1, *, tm, seq, dn_alpha):
    T, D = x2.shape
    pwid = p.shape[1]
    hb = tm // POOL_HALO
    kern = functools.partial(_merge_kernel, tm=tm, seq=seq, dn_alpha=dn_alpha)
    return pl.pallas_call(
        kern,
        grid=(T // tm,),
        in_specs=[
            pl.BlockSpec((tm, o.shape[1]), lambda i: (i, 0)),
            pl.BlockSpec((tm, pwid), lambda i: (i, 0)),
            pl.BlockSpec((POOL_HALO, pwid), lambda i: (jnp.maximum(i * hb - 1, 0), 0)),
            pl.BlockSpec((tm, gates.shape[1]), lambda i: (i, 0)),
            pl.BlockSpec((tm, D), lambda i: (i, 0)),
            _full(wap.shape), _full(pw.shape), _full(ps.shape), _full(wpp.shape), _full(wout.shape),
            _full(g1.shape), _full(b1.shape),
        ],
        out_specs=pl.BlockSpec((tm, D), lambda i: (i, 0)),
        out_shape=jax.ShapeDtypeStruct((T, D), F32),
        scratch_shapes=[pltpu.VMEM((POOL_HALO + tm, pwid), F32)],
        compiler_params=_cparams(("arbitrary",)),
        name="merge_ln1",
    )(o, p, p, gates, x2, wap, pw, ps, wpp, wout, g1, b1)


def _top_rows(s, k, payload=None):
    n = s.shape[0]
    row = lax.broadcasted_iota(jnp.int32, s.shape, 0)
    vals, picks = [], []
    for _ in range(k):
        m = jnp.max(s, axis=0, keepdims=True)
        am = jnp.min(jnp.where(s == m, row, n), axis=0, keepdims=True)
        hit = row == am
        vals.append(m)
        if payload is None:
            picks.append(am)
        else:
            picks.append(jnp.sum(jnp.where(hit, payload, 0), axis=0, keepdims=True))
        s = jnp.where(hit, -jnp.inf, s)
    return jnp.concatenate(vals, axis=0), jnp.concatenate(picks, axis=0)


def _packed_candidates(s1, i1, s2, i2, sub):
    kk = s1.shape[0]
    segs, start = [], 0
    for a in range(kk):
        segs.append((a, start, kk // (a + 1)))
        start += kk // (a + 1)
    total = start
    cand, eid = [], []
    for lo in range(0, total, SUBLANES):
        cv = ev = None
        for a, st, nb in segs:
            if st + nb <= lo or st >= lo + SUBLANES:
                continue
            if st <= lo and nb > SUBLANES:
                src_s, src_i = s2[lo - st:lo - st + SUBLANES], i2[lo - st:lo - st + SUBLANES]
            else:
                amt = (st - lo) % SUBLANES
                src_s = pltpu.roll(s2[:SUBLANES], amt, axis=0) if amt else s2[:SUBLANES]
                src_i = pltpu.roll(i2[:SUBLANES], amt, axis=0) if amt else i2[:SUBLANES]
            val = s1[a:a + 1] + src_s
            idv = i1[a:a + 1] * PEER_KEYS + src_i
            p0 = max(st - lo, 0)
            cv = val if p0 == 0 else jnp.where(sub >= p0, val, cv)
            ev = idv if p0 == 0 else jnp.where(sub >= p0, idv, ev)
        if total - lo < SUBLANES:
            cv = jnp.where(sub >= total - lo, -jnp.inf, cv)
        cand.append(cv)
        eid.append(ev)
    return jnp.concatenate(cand, axis=0), jnp.concatenate(eid, axis=0)


def _route_kernel(y_ref, wq_ref, sk_ref, idx_ref, gate_ref):
    yb = y_ref[...].astype(BF16)
    tm = yb.shape[0]
    nt = (((1,), (1,)), ((), ()))
    kk = PEER_TOPK
    sub = lax.broadcasted_iota(jnp.int32, (SUBLANES, tm), 0)

    def head(h, carry):
        q = jnp.dot(yb, wq_ref[h], preferred_element_type=F32).astype(BF16)
        top = []
        for part in range(2):
            qp = q[:, part * PEER_HALF:(part + 1) * PEER_HALF]
            sT = lax.dot_general(sk_ref[2 * h + part], qp, nt, preferred_element_type=F32)
            top.append(_top_rows(sT, kk))
        (s1, i1), (s2, i2) = top
        cand, eid = _packed_candidates(s1, i1, s2, i2, sub)
        sc, ids = _top_rows(cand, kk, payload=eid)
        e = jnp.exp(sc - sc[0:1])
        gate = e / jnp.sum(e, axis=0, keepdims=True)
        r0 = pl.multiple_of(h * kk, kk)
        idx_ref[pl.ds(r0, kk), :] = ids
        gate_ref[pl.ds(r0, kk), :] = gate
        return carry

    lax.fori_loop(0, PEER_HEADS, head, 0)


def _route(h1, wq3, sk, *, tm):
    T, D = h1.shape
    return pl.pallas_call(
        _route_kernel,
        grid=(T // tm,),
        in_specs=[pl.BlockSpec((tm, D), lambda i: (i, 0)), _full(wq3.shape), _full(sk.shape)],
        out_specs=[pl.BlockSpec((PEER_SLOTS, tm), lambda i: (0, i)),
                   pl.BlockSpec((PEER_SLOTS, tm), lambda i: (0, i))],
        out_shape=[jax.ShapeDtypeStruct((PEER_SLOTS, T), jnp.int32),
                   jax.ShapeDtypeStruct((PEER_SLOTS, T), F32)],
        compiler_params=_cparams(("arbitrary",)),
        name="peer_route",
    )(h1, wq3, sk)


def _pack_table(t):
    n, dd = t.shape
    bits = lax.bitcast_convert_type(t, jnp.uint32)
    b16 = (bits + jnp.uint32(0x7FFF) + ((bits >> 16) & jnp.uint32(1))) >> 16
    return (b16[:, :dd // 2] | (b16[:, dd // 2:] << 16)).reshape(n * (dd // (2 * LANES)), LANES)


def _unpack_words(w):
    lo = lax.bitcast_convert_type(w << 16, F32)
    hi = lax.bitcast_convert_type(w & jnp.uint32(0xFFFF0000), F32)
    return lo, hi


def _gather_rows(idx_ref, tab_ref, rs, c, dst):
    irow = idx_ref.at[c]
    for e in range(PEER_SLOTS):
        r0 = pl.multiple_of(irow[e], rs)
        dst[pl.ds(e, rs, stride=CHUNK_STRIDE), :] = tab_ref[pl.ds(r0, rs), :]


def _token_pipeline(tb, gather, fold, ca_ref, cb_ref):
    gather(0, ca_ref)

    def pair(kp, carry):
        c0 = 2 * kp
        gather(c0 + 1, cb_ref)
        carry = fold(c0, ca_ref, carry)
        gather(jnp.minimum(c0 + 2, tb - 1), ca_ref)
        return fold(c0 + 1, cb_ref, carry)

    return pair


def _score_kernel(idx_ref, y_ref, gate_ref, tab_ref, w_ref, ca_ref, cb_ref, *, tb):
    nsl = PEER_SLOTS
    rs = y_ref.shape[1] // 2
    st = CHUNK_STRIDE
    half = rs * LANES
    lane = lax.broadcasted_iota(jnp.int32, (nsl, tb), 1)

    def place(c, tot):
        w_ref[...] = jnp.where(lane == c, jnp.sum(tot, axis=1, keepdims=True), w_ref[...])

    def fold(c, src, pend):
        place(c - 1, pend)
        tot = None
        yv = y_ref[c]
        for s in range(rs):
            lo, hi = _unpack_words(src[s * st:s * st + nsl, :])
            part = lo * yv[s:s + 1] + hi * yv[rs + s:rs + s + 1]
            tot = part if tot is None else tot + part
        return tot

    w_ref[...] = jnp.zeros(w_ref.shape, F32)
    gather = functools.partial(_gather_rows, idx_ref, tab_ref, rs)
    pair = _token_pipeline(tb, gather, fold, ca_ref, cb_ref)
    pend = lax.fori_loop(0, tb // 2, pair, jnp.zeros((nsl, LANES), F32))
    place(tb - 1, pend)
    act = w_ref[...]
    gelu = 0.5 * act * (1.0 + lax.erf(act * (2.0 ** -0.5)))
    w_ref[...] = gate_ref[...] * gelu


def _table_spec(tab):
    return pl.BlockSpec(tab.shape, lambda i: (0, 0), pipeline_mode=pl.Buffered(1))


def _chunk_scratch(rs):
    return pltpu.VMEM((CHUNK_STRIDE * rs, LANES), jnp.uint32)


def _expert_scores(idx, y3, gateT, utab, *, tb):
    T = y3.shape[0]
    kern = functools.partial(_score_kernel, tb=tb)
    return pl.pallas_call(
        kern,
        grid=(T // tb,),
        in_specs=[
            pl.BlockSpec((tb, PEER_SLOTS), lambda i: (i, 0), memory_space=pltpu.SMEM),
            pl.BlockSpec((tb,) + y3.shape[1:], lambda i: (i, 0, 0)),
            pl.BlockSpec((PEER_SLOTS, tb), lambda i: (0, i)),
            _table_spec(utab),
        ],
        out_specs=pl.BlockSpec((PEER_SLOTS, tb), lambda i: (0, i)),
        out_shape=jax.ShapeDtypeStruct((PEER_SLOTS, T), F32),
        scratch_shapes=[_chunk_scratch(y3.shape[1] // 2)] * 2,
        compiler_params=_cparams(("arbitrary",)),
        name="peer_scores",
    )(idx, y3, gateT, utab)


def _mix_kernel(idx_ref, w_ref, tab_ref, f_ref, ca_ref, cb_ref, *, tb):
    nsl = PEER_SLOTS
    rs = f_ref.shape[1] // 2
    st = CHUNK_STRIDE
    half = rs * LANES
    lane = lax.broadcasted_iota(jnp.int32, (nsl, tb), 1)

    def fold(c, src, carry):
        wcol = jnp.sum(jnp.where(lane == c, w_ref[...], 0.0), axis=1, keepdims=True)
        wb = jnp.broadcast_to(wcol, (nsl, LANES))
        los, his = [], []
        for s in range(rs):
            lo, hi = _unpack_words(src[s * st:s * st + nsl, :])
            los.append(jnp.sum(lo * wb, axis=0, keepdims=True))
            his.append(jnp.sum(hi * wb, axis=0, keepdims=True))
        f_ref[c] = jnp.concatenate(los + his, axis=0)
        return carry

    gather = functools.partial(_gather_rows, idx_ref, tab_ref, rs)
    pair = _token_pipeline(tb, gather, fold, ca_ref, cb_ref)
    lax.fori_loop(0, tb // 2, pair, 0)


def _expert_mix(idx, wT, vtab, *, tb, rs):
    T = idx.shape[0]
    rows = 2 * rs
    kern = functools.partial(_mix_kernel, tb=tb)
    return pl.pallas_call(
        kern,
        grid=(T // tb,),
        in_specs=[
            pl.BlockSpec((tb, PEER_SLOTS), lambda i: (i, 0), memory_space=pltpu.SMEM),
            pl.BlockSpec((PEER_SLOTS, tb), lambda i: (0, i)),
            _table_spec(vtab),
        ],
        out_specs=pl.BlockSpec((tb, rows, LANES), lambda i: (i, 0, 0)),
        out_shape=jax.ShapeDtypeStruct((T, rows, LANES), F32),
        scratch_shapes=[_chunk_scratch(rs)] * 2,
        compiler_params=_cparams(("arbitrary",)),
        name="peer_mix",
    )(idx, wT, vtab)


def _ln2_kernel(h_ref, f_ref, g_ref, b_ref, o_ref, *, dn_alpha):
    o_ref[...] = _layer_norm_rows(dn_alpha * h_ref[...] + f_ref[...], g_ref[...], b_ref[...])


def _residual_ln(h1, ffn, g, b, *, tm, dn_alpha):
    T, D = h1.shape
    return pl.pallas_call(
        functools.partial(_ln2_kernel, dn_alpha=dn_alpha),
        grid=(T // tm,),
        in_specs=[pl.BlockSpec((tm, D), lambda i: (i, 0)), pl.BlockSpec((tm, D), lambda i: (i, 0)),
                  _full(g.shape), _full(b.shape)],
        out_specs=pl.BlockSpec((tm, D), lambda i: (i, 0)),
        out_shape=jax.ShapeDtypeStruct((T, D), F32),
        compiler_params=_cparams(("arbitrary",)),
        name="residual_ln2",
    )(h1, ffn, g, b)


def _layer(h2d, B, S, depth, lam_init, w_in, b_gate, lq1, lk1, lq2, lk2, subln_g, w_attn_proj, pool_w,
           pool_scale, w_pool_proj, w_out, ln1_g, ln1_b, peer_wq, peer_subkeys, peer_u, peer_v,
           ln2_g, ln2_b):
    T, D = h2d.shape
    dn_alpha = (2.0 * depth) ** 0.25
    wqk = 2 * N_HEADS * HEAD_DIM
    wv = N_HEADS * V_HEAD_DIM
    wpool = len(POOL_WINDOWS) * POOL_GROUP_DIM
    tm = min(PROJ_ROWS, S)
    ta = min(ATTN_BLOCK, S // 2)
    c0, c1, c2, c3 = wqk, 2 * wqk, 2 * wqk + wv, 2 * wqk + wv + wpool
    wb = w_in.astype(BF16)
    row = lambda a: a.reshape(1, -1).astype(F32)

    qT, k, vTb, p, gates = _project(
        h2d, wb[:, :c0].T, wb[:, c0:c1], wb[:, c1:c2].T, wb[:, c2:c3], wb[:, c3:], row(b_gate),
        tm=tm, ta=ta, scale=HEAD_DIM ** -0.5 * LOG2E)

    slopes = jnp.asarray(np.array([2.0 ** (-8.0 * (h + 1) / N_HEADS) * LOG2E for h in range(N_HEADS)],
                                  dtype=np.float32))
    lamp = jnp.stack([lq1, lk1, lq2, lk2]).astype(F32)
    o = _attention(slopes, lamp, row(subln_g), qT, k, vTb, B=B, S=S, tq=ta, tk=ta, lam_init=lam_init)

    h1 = _merge(o, p, gates, h2d, w_attn_proj.astype(BF16), pool_w.astype(BF16), row(pool_scale),
                w_pool_proj.astype(BF16), w_out.astype(BF16), row(ln1_g), row(ln1_b),
                tm=tm, seq=S, dn_alpha=dn_alpha)

    wq3 = peer_wq.astype(BF16).reshape(D, PEER_HEADS, 2 * PEER_HALF).transpose(1, 0, 2)
    sk = peer_subkeys.astype(BF16).reshape(PEER_HEADS * 2, PEER_KEYS, PEER_HALF)
    idxT, gateT = _route(h1, wq3, sk, tm=min(ROUTE_ROWS, T))

    tb = min(PEER_ROWS, T)
    rs = D // (2 * LANES)
    idx = idxT.T * rs
    wT = _expert_scores(idx, h1.reshape(T, D // LANES, LANES), gateT, _pack_table(peer_u), tb=tb)
    ffn = _expert_mix(idx, wT, _pack_table(peer_v), tb=tb, rs=rs)
    return _residual_ln(h1, ffn.reshape(T, D), row(ln2_g), row(ln2_b), tm=tm, dn_alpha=dn_alpha)


def kernel(x, w_in, b_gate, lambda_q1, lambda_k1, lambda_q2, lambda_k2, subln_g, w_attn_proj, pool_w,
           pool_scale, w_pool_proj, w_out, ln1_g, ln1_b, peer_wq, peer_subkeys, peer_u, peer_v,
           ln2_g, ln2_b):
    B, S, D = x.shape
    depth = w_in.shape[0]
    h = x.reshape(B * S, D)
    for l in range(depth):
        lam_init = 0.8 - 0.6 * math.exp(-0.3 * l)
        h = _layer(h, B, S, depth, lam_init, w_in[l], b_gate[l], lambda_q1[l], lambda_k1[l],
                   lambda_q2[l], lambda_k2[l], subln_g[l], w_attn_proj[l], pool_w[l], pool_scale[l],
                   w_pool_proj[l], w_out[l], ln1_g[l], ln1_b[l], peer_wq[l], peer_subkeys[l],
                   peer_u[l], peer_v[l], ln2_g[l], ln2_b[l])
    return h.reshape(B, S, D)
```

```python
import functools
import math

import jax
import jax.numpy as jnp
import numpy as np
from jax import lax
from jax.experimental import pallas as pl
from jax.experimental.pallas import tpu as pltpu

F32 = jnp.float32
BF16 = jnp.bfloat16

N_HEADS = 8
HEAD_DIM = 64
V_HEAD_DIM = 2 * HEAD_DIM
POOL_WINDOWS = (2, 4, 8, 16)
POOL_GROUP_DIM = 128
POOL_HALO = 16
PEER_HEADS = 8
PEER_KEYS = 128
PEER_HALF = 128
PEER_TOPK = 16
PEER_SLOTS = PEER_HEADS * PEER_TOPK
LN_EPS = 1e-5
LOG2E = math.log2(math.e)

LANES = 128
SUBLANES = 8
VMEM_LIMIT_BYTES = 56 * 1024 * 1024
PROJ_ROWS = 512
ATTN_BLOCK = 256
ROUTE_ROWS = 2048
PEER_ROWS = 128
CHUNK_STRIDE = PEER_SLOTS + 1
DENOM_ROWS = 16
PAIRS_PER_TRIP = 4


def _cparams(sem):
    return pltpu.CompilerParams(dimension_semantics=sem, vmem_limit_bytes=VMEM_LIMIT_BYTES)


def _full(shape):
    n = len(shape)
    return pl.BlockSpec(shape, lambda *_: (0,) * n)


def _proj_kernel(x_ref, wqT_ref, wk_ref, wvT_ref, wp_ref, wg_ref, bg_ref,
                 qT_ref, k_ref, vT_ref, p_ref, gate_ref, *, scale, n_vblk, ta):
    xb = x_ref[...].astype(BF16)
    nt = (((1,), (1,)), ((), ()))
    qT = lax.dot_general(wqT_ref[...], xb, nt, preferred_element_type=F32)
    qT_ref[...] = (qT * scale).astype(BF16)
    k_ref[...] = jnp.dot(xb, wk_ref[...], preferred_element_type=F32).astype(BF16)
    vT = lax.dot_general(wvT_ref[...], xb, nt, preferred_element_type=F32).astype(BF16)
    vd, hr = V_HEAD_DIM, V_HEAD_DIM + DENOM_ROWS
    ones_rows = jnp.where(lax.broadcasted_iota(jnp.int32, (DENOM_ROWS, ta), 0) == 0, 1.0, 0.0).astype(BF16)
    for n in range(n_vblk):
        for h in range(N_HEADS):
            vT_ref[n, h * hr:h * hr + vd, :] = vT[h * vd:(h + 1) * vd, n * ta:(n + 1) * ta]
            vT_ref[n, h * hr + vd:(h + 1) * hr, :] = ones_rows
    p_ref[...] = jnp.dot(xb, wp_ref[...], preferred_element_type=F32)
    g = jnp.dot(xb, wg_ref[...], preferred_element_type=F32) + bg_ref[...]
    gate_ref[...] = jax.nn.sigmoid(g).astype(BF16)


def _project(x2, wqT, wk, wvT, wp, wg, bg, *, tm, ta, scale):
    T, D = x2.shape
    wqk, wv, wpool, wgate = wqT.shape[0], wvT.shape[0], wp.shape[1], wg.shape[1]
    n_vblk = tm // ta
    wvx = N_HEADS * (V_HEAD_DIM + DENOM_ROWS)
    kern = functools.partial(_proj_kernel, scale=scale, n_vblk=n_vblk, ta=ta)
    return pl.pallas_call(
        kern,
        grid=(T // tm,),
        in_specs=[
            pl.BlockSpec((tm, D), lambda i: (i, 0)),
            _full(wqT.shape), _full(wk.shape), _full(wvT.shape), _full(wp.shape), _full(wg.shape),
            _full(bg.shape),
        ],
        out_specs=[
            pl.BlockSpec((wqk, tm), lambda i: (0, i)),
            pl.BlockSpec((tm, wqk), lambda i: (i, 0)),
            pl.BlockSpec((n_vblk, wvx, ta), lambda i: (i, 0, 0)),
            pl.BlockSpec((tm, wpool), lambda i: (i, 0)),
            pl.BlockSpec((tm, wgate), lambda i: (i, 0)),
        ],
        out_shape=[
            jax.ShapeDtypeStruct((wqk, T), BF16),
            jax.ShapeDtypeStruct((T, wqk), BF16),
            jax.ShapeDtypeStruct((T // ta, wvx, ta), BF16),
            jax.ShapeDtypeStruct((T, wpool), F32),
            jax.ShapeDtypeStruct((T, wgate), BF16),
        ],
        compiler_params=_cparams(("arbitrary",)),
        name="input_projection",
    )(x2, wqT, wk, wvT, wp, wg, bg)


def _attn_kernel(slopes_ref, lamp_ref, g_ref, qT_ref, k_ref, vT_ref, o_ref,
                 m_ref, acc_ref, al_ref, base_ref, s0_ref, s1_ref, cm0_ref, cm1_ref, p0_ref, p1_ref,
                 *, tq, tk, lam_init):
    h = pl.program_id(1)
    i = pl.program_id(2)
    slope = slopes_ref[h]
    d = HEAD_DIM
    nrep = 2 * tq // LANES
    qT = qT_ref[...]
    z = jnp.zeros((d, tq), BF16)
    qq = jnp.concatenate([jnp.concatenate([qT[:d], z], axis=0),
                          jnp.concatenate([z, qT[d:]], axis=0)], axis=1)
    m_ref[...] = jnp.full(m_ref.shape, -jnp.inf, F32)
    acc_ref[...] = jnp.zeros(acc_ref.shape, F32)
    al_ref[...] = jnp.ones(al_ref.shape, F32)
    p1_ref[...] = jnp.zeros(p1_ref.shape, BF16)
    base_ref[...] = lax.broadcasted_iota(jnp.int32, (tk, LANES), 0).astype(F32) * slope

    def scores(j, s_ref, cm_ref):
        kb = k_ref[pl.ds(pl.multiple_of(j * tk, tk), tk), :]
        t = jnp.dot(kb, qq, preferred_element_type=F32) + jnp.concatenate([base_ref[...]] * nrep, axis=1)
        s_ref[...] = t
        cm_ref[...] = jnp.max(t, axis=0, keepdims=True)

    def softmax(j, s_ref, cm_ref, p_ref, masked):
        off = (j * tk - i * tq).astype(F32) * slope
        t = s_ref[...]
        if masked:
            key = j * tk + lax.broadcasted_iota(jnp.int32, (tk, 2 * tq), 0)
            col = lax.broadcasted_iota(jnp.int32, (tk, 2 * tq), 1)
            qry = i * tq + jnp.where(col >= tq, col - tq, col)
            t = jnp.where(key > qry, -jnp.inf, t)
            cmax = jnp.max(t, axis=0, keepdims=True)
        else:
            cmax = cm_ref[...]
        m_old = m_ref[...]
        m_new = jnp.maximum(m_old, cmax + off)
        alpha = jnp.exp2(m_old - m_new)
        pr = jnp.exp2(t + (off - m_new))
        p_ref[...] = pr.astype(BF16)
        m_ref[...] = m_new
        return alpha

    def values(j, p_ref):
        vb = vT_ref[jnp.maximum(j, 0)]
        acc_ref[...] = al_ref[...] * acc_ref[...] + jnp.dot(vb, p_ref[...],
                                                             preferred_element_type=F32)

    def half(j, cur, nxt, p_cur, p_prv, masked, prefetch):
        if prefetch:
            scores(j + 1, *nxt)
        alpha = softmax(j, *cur, p_cur, masked)
        values(j - 1, p_prv)
        al_ref[...] = alpha

    def pair(t, masked, last):
        j0 = 2 * t
        half(j0, buf0, buf1, p0_ref, p1_ref, masked, True)
        half(j0 + 1, buf1, buf0, p1_ref, p0_ref, masked, not last)

    buf0, buf1 = (s0_ref, cm0_ref), (s1_ref, cm1_ref)
    scores(0, *buf0)

    def body(t, carry):
        for u in range(PAIRS_PER_TRIP):
            pair(PAIRS_PER_TRIP * t + u, False, False)
        return carry

    npairs = (i * tq) // (2 * tk)
    ntrips = npairs // PAIRS_PER_TRIP
    lax.fori_loop(0, ntrips, body, 0)
    done = ntrips * PAIRS_PER_TRIP
    step = PAIRS_PER_TRIP // 2
    while step:
        take = ((npairs - done) & step) != 0

        @pl.when(take)
        def _(done=done, step=step):
            for u in range(step):
                pair(done + u, False, False)

        done = done + jnp.where(take, step, 0)
        step //= 2

    pair(npairs, True, True)
    values(2 * npairs + 1, p1_ref)

    lamp = lamp_ref[...]
    lam = (jnp.exp(jnp.sum(lamp[0:1] * lamp[1:2], axis=1, keepdims=True))
           - jnp.exp(jnp.sum(lamp[2:3] * lamp[3:4], axis=1, keepdims=True)) + lam_init)
    acc = acc_ref[0:2 * d, :]
    l = acc_ref[2 * d:2 * d + 1, :]
    oT = acc[:, :tq] / l[:, :tq] - lam * (acc[:, tq:] / l[:, tq:])
    ms = jnp.mean(oT * oT, axis=0, keepdims=True)
    oT = oT * lax.rsqrt(ms + LN_EPS)
    o = oT.T * g_ref[...] * (1.0 - lam_init)
    o_ref[...] = o.astype(BF16)


def _attention(slopes, lamp, subln_g, qT, k, vTb, *, B, S, tq, tk, lam_init):
    T = B * S
    nq = S // tq
    nk = S // tk
    assert tq in (tk, 2 * tk) and nk % 2 == 0
    vd = V_HEAD_DIM
    kern = functools.partial(_attn_kernel, tq=tq, tk=tk, lam_init=lam_init)
    return pl.pallas_call(
        kern,
        grid=(B, N_HEADS, nq),
        in_specs=[
            pl.BlockSpec(memory_space=pltpu.SMEM),
            _full(lamp.shape),
            _full(subln_g.shape),
            pl.BlockSpec((vd, tq), lambda b, h, i: (h, b * nq + i)),
            pl.BlockSpec((S, vd), lambda b, h, i: (b, h)),
            pl.BlockSpec((nk, vd + DENOM_ROWS, tk), lambda b, h, i: (b, h, 0)),
        ],
        out_specs=pl.BlockSpec((tq, vd), lambda b, h, i: (b * nq + i, h)),
        out_shape=jax.ShapeDtypeStruct((T, N_HEADS * vd), BF16),
        scratch_shapes=[
            pltpu.VMEM((1, 2 * tq), F32),
            pltpu.VMEM((vd + DENOM_ROWS, 2 * tq), F32),
            pltpu.VMEM((1, 2 * tq), F32),
            pltpu.VMEM((tk, LANES), F32),
            pltpu.VMEM((tk, 2 * tq), F32),
            pltpu.VMEM((tk, 2 * tq), F32),
            pltpu.VMEM((1, 2 * tq), F32),
            pltpu.VMEM((1, 2 * tq), F32),
            pltpu.VMEM((tk, 2 * tq), BF16),
            pltpu.VMEM((tk, 2 * tq), BF16),
        ],
        compiler_params=_cparams(("arbitrary", "arbitrary", "arbitrary")),
        name="diff_attention",
    )(slopes, lamp, subln_g, qT, k, vTb)


def _layer_norm_rows(r, g, b):
    mu = jnp.mean(r, axis=-1, keepdims=True)
    dlt = r - mu
    var = jnp.mean(dlt * dlt, axis=-1, keepdims=True)
    return dlt * lax.rsqrt(var + LN_EPS) * g + b


def _merge_kernel(o_ref, p_ref, ph_ref, gate_ref, x_ref, wap_ref, pw_ref, ps_ref, wpp_ref,
                  wout_ref, g1_ref, b1_ref, h1_ref, ext_ref, *, tm, seq, dn_alpha):
    i = pl.program_id(0)
    t0 = (i * tm) % seq
    hl = POOL_HALO
    ext_ref[0:hl, :] = jnp.where(t0 == 0, 0.0, ph_ref[...])
    ext_ref[hl:hl + tm, :] = p_ref[...]
    pos = (t0 + lax.broadcasted_iota(jnp.int32, (tm, POOL_GROUP_DIM), 0)).astype(F32)
    pm = []
    for gi, w in enumerate(POOL_WINDOWS):
        sl = slice(gi * POOL_GROUP_DIM, (gi + 1) * POOL_GROUP_DIM)
        cur = ext_ref[hl:hl + tm, sl]
        win = cur
        for back in range(1, w):
            win = win + ext_ref[hl - back:hl - back + tm, sl]
        cnt = jnp.minimum(float(w), pos + 1.0)
        pooled = win / cnt - cur
        pm.append(jnp.dot(pooled.astype(BF16), pw_ref[gi], preferred_element_type=F32))
    pm = jnp.concatenate(pm, axis=1) * ps_ref[...]
    pool_branch = jnp.dot(pm.astype(BF16), wpp_ref[...], preferred_element_type=F32)
    attn_branch = jnp.dot(o_ref[...], wap_ref[...], preferred_element_type=F32)
    dm = attn_branch.shape[1]
    gates = gate_ref[...].astype(F32)
    merged = gates[:, :dm] * attn_branch + gates[:, dm:] * pool_branch
    mix = jnp.dot(merged.astype(BF16), wout_ref[...], preferred_element_type=F32)
    h1_ref[...] = _layer_norm_rows(dn_alpha * x_ref[...] + mix, g1_ref[...], b1_ref[...])


def _merge(o, p, gates, x2, wap, pw, ps, wpp, wout, g1, b1, *, tm, seq, dn_alpha):
    T, D = x2.shape
    pwid = p.shape[1]
    hb = tm // POOL_HALO
    kern = functools.partial(_merge_kernel, tm=tm, seq=seq, dn_alpha=dn_alpha)
    return pl.pallas_call(
        kern,
        grid=(T // tm,),
        in_specs=[
            pl.BlockSpec((tm, o.shape[1]), lambda i: (i, 0)),
            pl.BlockSpec((tm, pwid), lambda i: (i, 0)),
            pl.BlockSpec((POOL_HALO, pwid), lambda i: (jnp.maximum(i * hb - 1, 0), 0)),
            pl.BlockSpec((tm, gates.shape[1]), lambda i: (i, 0)),
            pl.BlockSpec((tm, D), lambda i: (i, 0)),
            _full(wap.shape), _full(pw.shape), _full(ps.shape), _full(wpp.shape), _full(wout.shape),
            _full(g1.shape), _full(b1.shape),
        ],
        out_specs=pl.BlockSpec((tm, D), lambda i: (i, 0)),
        out_shape=jax.ShapeDtypeStruct((T, D), F32),
        scratch_shapes=[pltpu.VMEM((POOL_HALO + tm, pwid), F32)],
        compiler_params=_cparams(("arbitrary",)),
        name="merge_ln1",
    )(o, p, p, gates, x2, wap, pw, ps, wpp, wout, g1, b1)


def _top_rows(s, k, payload=None):
    n, t = s.shape
    ng = n // SUBLANES
    sub = lax.broadcasted_iota(jnp.int32, (SUBLANES, t), 0)
    tiles = [s[g * SUBLANES:(g + 1) * SUBLANES] for g in range(ng)]
    rows = [sub + g * SUBLANES for g in range(ng)]
    vals, picks = [], []
    for _ in range(k):
        best, first = tiles[0], jnp.zeros((SUBLANES, t), jnp.int32)
        for g in range(1, ng):
            first = jnp.where(tiles[g] > best, g, first)
            best = jnp.maximum(best, tiles[g])
        m = jnp.max(best, axis=0, keepdims=True)
        am = jnp.min(jnp.where(best == m, first * SUBLANES + sub, n), axis=0, keepdims=True)
        hits = [rows[g] == am for g in range(ng)]
        vals.append(m)
        if payload is None:
            picks.append(am)
        else:
            found = [jnp.where(hits[g], payload[g * SUBLANES:(g + 1) * SUBLANES], 0) for g in range(ng)]
            picks.append(jnp.sum(functools.reduce(jnp.add, found), axis=0, keepdims=True))
        tiles = [jnp.where(hits[g], -jnp.inf, tiles[g]) for g in range(ng)]
    return jnp.concatenate(vals, axis=0), jnp.concatenate(picks, axis=0)


def _packed_candidates(s1, i1, s2, i2, sub):
    kk = s1.shape[0]
    segs, start = [], 0
    for a in range(kk):
        segs.append((a, start, kk // (a + 1)))
        start += kk // (a + 1)
    total = start
    cand, eid = [], []
    for lo in range(0, total, SUBLANES):
        cv = ev = None
        for a, st, nb in segs:
            if st + nb <= lo or st >= lo + SUBLANES:
                continue
            if st <= lo and nb > SUBLANES:
                src_s, src_i = s2[lo - st:lo - st + SUBLANES], i2[lo - st:lo - st + SUBLANES]
            else:
                amt = (st - lo) % SUBLANES
                src_s = pltpu.roll(s2[:SUBLANES], amt, axis=0) if amt else s2[:SUBLANES]
                src_i = pltpu.roll(i2[:SUBLANES], amt, axis=0) if amt else i2[:SUBLANES]
            val = s1[a:a + 1] + src_s
            idv = i1[a:a + 1] * PEER_KEYS + src_i
            p0 = max(st - lo, 0)
            cv = val if p0 == 0 else jnp.where(sub >= p0, val, cv)
            ev = idv if p0 == 0 else jnp.where(sub >= p0, idv, ev)
        if total - lo < SUBLANES:
            cv = jnp.where(sub >= total - lo, -jnp.inf, cv)
        cand.append(cv)
        eid.append(ev)
    return jnp.concatenate(cand, axis=0), jnp.concatenate(eid, axis=0)


def _route_kernel(y_ref, wq_ref, sk_ref, idx_ref, gate_ref):
    yb = y_ref[...].astype(BF16)
    tm = yb.shape[0]
    nt = (((1,), (1,)), ((), ()))
    kk = PEER_TOPK
    sub = lax.broadcasted_iota(jnp.int32, (SUBLANES, tm), 0)

    def head(h, carry):
        q = jnp.dot(yb, wq_ref[h], preferred_element_type=F32).astype(BF16)
        top = []
        for part in range(2):
            qp = q[:, part * PEER_HALF:(part + 1) * PEER_HALF]
            sT = lax.dot_general(sk_ref[2 * h + part], qp, nt, preferred_element_type=F32)
            top.append(_top_rows(sT, kk))
        (s1, i1), (s2, i2) = top
        cand, eid = _packed_candidates(s1, i1, s2, i2, sub)
        sc, ids = _top_rows(cand, kk, payload=eid)
        e = jnp.exp(sc - sc[0:1])
        gate = e / jnp.sum(e, axis=0, keepdims=True)
        r0 = pl.multiple_of(h * kk, kk)
        idx_ref[pl.ds(r0, kk), :] = ids
        gate_ref[pl.ds(r0, kk), :] = gate
        return carry

    lax.fori_loop(0, PEER_HEADS, head, 0)


def _route(h1, wq3, sk, *, tm):
    T, D = h1.shape
    return pl.pallas_call(
        _route_kernel,
        grid=(T // tm,),
        in_specs=[pl.BlockSpec((tm, D), lambda i: (i, 0)), _full(wq3.shape), _full(sk.shape)],
        out_specs=[pl.BlockSpec((PEER_SLOTS, tm), lambda i: (0, i)),
                   pl.BlockSpec((PEER_SLOTS, tm), lambda i: (0, i))],
        out_shape=[jax.ShapeDtypeStruct((PEER_SLOTS, T), jnp.int32),
                   jax.ShapeDtypeStruct((PEER_SLOTS, T), F32)],
        compiler_params=_cparams(("arbitrary",)),
        name="peer_route",
    )(h1, wq3, sk)


def _pack_table(t):
    n, dd = t.shape
    bits = lax.bitcast_convert_type(t, jnp.uint32)
    b16 = (bits + jnp.uint32(0x7FFF) + ((bits >> 16) & jnp.uint32(1))) >> 16
    return (b16[:, :dd // 2] | (b16[:, dd // 2:] << 16)).reshape(n * (dd // (2 * LANES)), LANES)


def _unpack_words(w):
    lo = lax.bitcast_convert_type(w << 16, F32)
    hi = lax.bitcast_convert_type(w & jnp.uint32(0xFFFF0000), F32)
    return lo, hi


def _gather_rows(idx_ref, tab_ref, rs, c, dst):
    irow = idx_ref.at[c]
    for e in range(PEER_SLOTS):
        r0 = pl.multiple_of(irow[e], rs)
        dst[pl.ds(e, rs, stride=CHUNK_STRIDE), :] = tab_ref[pl.ds(r0, rs), :]


def _token_pipeline(tb, gather, fold, ca_ref, cb_ref):
    gather(0, ca_ref)

    def pair(kp, carry):
        c0 = 2 * kp
        gather(c0 + 1, cb_ref)
        carry = fold(c0, ca_ref, carry)
        gather(jnp.minimum(c0 + 2, tb - 1), ca_ref)
        return fold(c0 + 1, cb_ref, carry)

    return pair


def _score_kernel(idx_ref, y_ref, gate_ref, tab_ref, w_ref, ca_ref, cb_ref, *, tb):
    nsl = PEER_SLOTS
    rs = y_ref.shape[1] // 2
    st = CHUNK_STRIDE
    half = rs * LANES
    lane = lax.broadcasted_iota(jnp.int32, (nsl, tb), 1)

    def place(c, tot):
        w_ref[...] = jnp.where(lane == c, jnp.sum(tot, axis=1, keepdims=True), w_ref[...])

    def fold(c, src, pend):
        place(c - 1, pend)
        tot = None
        yv = y_ref[c]
        for s in range(rs):
            lo, hi = _unpack_words(src[s * st:s * st + nsl, :])
            part = lo * yv[s:s + 1] + hi * yv[rs + s:rs + s + 1]
            tot = part if tot is None else tot + part
        return tot

    w_ref[...] = jnp.zeros(w_ref.shape, F32)
    gather = functools.partial(_gather_rows, idx_ref, tab_ref, rs)
    pair = _token_pipeline(tb, gather, fold, ca_ref, cb_ref)
    pend = lax.fori_loop(0, tb // 2, pair, jnp.zeros((nsl, LANES), F32))
    place(tb - 1, pend)
    act = w_ref[...]
    gelu = 0.5 * act * (1.0 + lax.erf(act * (2.0 ** -0.5)))
    w_ref[...] = gate_ref[...] * gelu


def _table_spec(tab):
    return pl.BlockSpec(tab.shape, lambda i: (0, 0), pipeline_mode=pl.Buffered(1))


def _chunk_scratch(rs):
    return pltpu.VMEM((CHUNK_STRIDE * rs, LANES), jnp.uint32)


def _expert_scores(idx, y3, gateT, utab, *, tb):
    T = y3.shape[0]
    kern = functools.partial(_score_kernel, tb=tb)
    return pl.pallas_call(
        kern,
        grid=(T // tb,),
        in_specs=[
            pl.BlockSpec((tb, PEER_SLOTS), lambda i: (i, 0), memory_space=pltpu.SMEM),
            pl.BlockSpec((tb,) + y3.shape[1:], lambda i: (i, 0, 0)),
            pl.BlockSpec((PEER_SLOTS, tb), lambda i: (0, i)),
            _table_spec(utab),
        ],
        out_specs=pl.BlockSpec((PEER_SLOTS, tb), lambda i: (0, i)),
        out_shape=jax.ShapeDtypeStruct((PEER_SLOTS, T), F32),
        scratch_shapes=[_chunk_scratch(y3.shape[1] // 2)] * 2,
        compiler_params=_cparams(("arbitrary",)),
        name="peer_scores",
    )(idx, y3, gateT, utab)


def _mix_kernel(idx_ref, w_ref, tab_ref, f_ref, ca_ref, cb_ref, *, tb):
    nsl = PEER_SLOTS
    rs = f_ref.shape[1] // 2
    st = CHUNK_STRIDE
    half = rs * LANES
    lane = lax.broadcasted_iota(jnp.int32, (nsl, tb), 1)

    def fold(c, src, carry):
        wcol = jnp.sum(jnp.where(lane == c, w_ref[...], 0.0), axis=1, keepdims=True)
        wb = jnp.broadcast_to(wcol, (nsl, LANES))
        los, his = [], []
        for s in range(rs):
            lo, hi = _unpack_words(src[s * st:s * st + nsl, :])
            los.append(jnp.sum(lo * wb, axis=0, keepdims=True))
            his.append(jnp.sum(hi * wb, axis=0, keepdims=True))
        f_ref[c] = jnp.concatenate(los + his, axis=0)
        return carry

    gather = functools.partial(_gather_rows, idx_ref, tab_ref, rs)
    pair = _token_pipeline(tb, gather, fold, ca_ref, cb_ref)
    lax.fori_loop(0, tb // 2, pair, 0)


def _expert_mix(idx, wT, vtab, *, tb, rs):
    T = idx.shape[0]
    rows = 2 * rs
    kern = functools.partial(_mix_kernel, tb=tb)
    return pl.pallas_call(
        kern,
        grid=(T // tb,),
        in_specs=[
            pl.BlockSpec((tb, PEER_SLOTS), lambda i: (i, 0), memory_space=pltpu.SMEM),
            pl.BlockSpec((PEER_SLOTS, tb), lambda i: (0, i)),
            _table_spec(vtab),
        ],
        out_specs=pl.BlockSpec((tb, rows, LANES), lambda i: (i, 0, 0)),
        out_shape=jax.ShapeDtypeStruct((T, rows, LANES), F32),
        scratch_shapes=[_chunk_scratch(rs)] * 2,
        compiler_params=_cparams(("arbitrary",)),
        name="peer_mix",
    )(idx, wT, vtab)


def _ln2_kernel(h_ref, f_ref, g_ref, b_ref, o_ref, *, dn_alpha):
    o_ref[...] = _layer_norm_rows(dn_alpha * h_ref[...] + f_ref[...], g_ref[...], b_ref[...])


def _residual_ln(h1, ffn, g, b, *, tm, dn_alpha):
    T, D = h1.shape
    return pl.pallas_call(
        functools.partial(_ln2_kernel, dn_alpha=dn_alpha),
        grid=(T // tm,),
        in_specs=[pl.BlockSpec((tm, D), lambda i: (i, 0)), pl.BlockSpec((tm, D), lambda i: (i, 0)),
                  _full(g.shape), _full(b.shape)],
        out_specs=pl.BlockSpec((tm, D), lambda i: (i, 0)),
        out_shape=jax.ShapeDtypeStruct((T, D), F32),
        compiler_params=_cparams(("arbitrary",)),
        name="residual_ln2",
    )(h1, ffn, g, b)


def _layer(h2d, B, S, depth, lam_init, w_in, b_gate, lq1, lk1, lq2, lk2, subln_g, w_attn_proj, pool_w,
           pool_scale, w_pool_proj, w_out, ln1_g, ln1_b, peer_wq, peer_subkeys, peer_u, peer_v,
           ln2_g, ln2_b):
    T, D = h2d.shape
    dn_alpha = (2.0 * depth) ** 0.25
    wqk = 2 * N_HEADS * HEAD_DIM
    wv = N_HEADS * V_HEAD_DIM
    wpool = len(POOL_WINDOWS) * POOL_GROUP_DIM
    tm = min(PROJ_ROWS, S)
    ta = min(ATTN_BLOCK, S // 2)
    c0, c1, c2, c3 = wqk, 2 * wqk, 2 * wqk + wv, 2 * wqk + wv + wpool
    wb = w_in.astype(BF16)
    row = lambda a: a.reshape(1, -1).astype(F32)

    qT, k, vTb, p, gates = _project(
        h2d, wb[:, :c0].T, wb[:, c0:c1], wb[:, c1:c2].T, wb[:, c2:c3], wb[:, c3:], row(b_gate),
        tm=tm, ta=ta, scale=HEAD_DIM ** -0.5 * LOG2E)

    slopes = jnp.asarray(np.array([2.0 ** (-8.0 * (h + 1) / N_HEADS) * LOG2E for h in range(N_HEADS)],
                                  dtype=np.float32))
    lamp = jnp.stack([lq1, lk1, lq2, lk2]).astype(F32)
    o = _attention(slopes, lamp, row(subln_g), qT, k, vTb, B=B, S=S, tq=ta, tk=ta, lam_init=lam_init)

    h1 = _merge(o, p, gates, h2d, w_attn_proj.astype(BF16), pool_w.astype(BF16), row(pool_scale),
                w_pool_proj.astype(BF16), w_out.astype(BF16), row(ln1_g), row(ln1_b),
                tm=tm, seq=S, dn_alpha=dn_alpha)

    wq3 = peer_wq.astype(BF16).reshape(D, PEER_HEADS, 2 * PEER_HALF).transpose(1, 0, 2)
    sk = peer_subkeys.astype(BF16).reshape(PEER_HEADS * 2, PEER_KEYS, PEER_HALF)
    idxT, gateT = _route(h1, wq3, sk, tm=min(ROUTE_ROWS, T))

    tb = min(PEER_ROWS, T)
    rs = D // (2 * LANES)
    idx = idxT.T * rs
    wT = _expert_scores(idx, h1.reshape(T, D // LANES, LANES), gateT, _pack_table(peer_u), tb=tb)
    ffn = _expert_mix(idx, wT, _pack_table(peer_v), tb=tb, rs=rs)
    return _residual_ln(h1, ffn.reshape(T, D), row(ln2_g), row(ln2_b), tm=tm, dn_alpha=dn_alpha)


def kernel(x, w_in, b_gate, lambda_q1, lambda_k1, lambda_q2, lambda_k2, subln_g, w_attn_proj, pool_w,
           pool_scale, w_pool_proj, w_out, ln1_g, ln1_b, peer_wq, peer_subkeys, peer_u, peer_v,
           ln2_g, ln2_b):
    B, S, D = x.shape
    depth = w_in.shape[0]
    h = x.reshape(B * S, D)
    for l in range(depth):
        lam_init = 0.8 - 0.6 * math.exp(-0.3 * l)
        h = _layer(h, B, S, depth, lam_init, w_in[l], b_gate[l], lambda_q1[l], lambda_k1[l],
                   lambda_q2[l], lambda_k2[l], subln_g[l], w_attn_proj[l], pool_w[l], pool_scale[l],
                   w_pool_proj[l], w_out[l], ln1_g[l], ln1_b[l], peer_wq[l], peer_subkeys[l],
                   peer_u[l], peer_v[l], ln2_g[l], ln2_b[l])
    return h.reshape(B, S, D)
```

```python
import functools
import math

import jax
import jax.numpy as jnp
import numpy as np
from jax import lax
from jax.experimental import pallas as pl
from jax.experimental.pallas import tpu as pltpu

F32 = jnp.float32
BF16 = jnp.bfloat16

N_HEADS = 8
HEAD_DIM = 64
V_HEAD_DIM = 2 * HEAD_DIM
POOL_WINDOWS = (2, 4, 8, 16)
POOL_GROUP_DIM = 128
POOL_HALO = 16
PEER_HEADS = 8
PEER_KEYS = 128
PEER_HALF = 128
PEER_TOPK = 16
PEER_SLOTS = PEER_HEADS * PEER_TOPK
LN_EPS = 1e-5
LOG2E = math.log2(math.e)

LANES = 128
SUBLANES = 8
VMEM_LIMIT_BYTES = 56 * 1024 * 1024
PROJ_ROWS = 512
ATTN_BLOCK = 256
ROUTE_ROWS = 2048
PEER_ROWS = 128
CHUNK_STRIDE = PEER_SLOTS + 1
DENOM_ROWS = 16
PAIRS_PER_TRIP = 8


def _cparams(sem):
    return pltpu.CompilerParams(dimension_semantics=sem, vmem_limit_bytes=VMEM_LIMIT_BYTES)


def _full(shape):
    n = len(shape)
    return pl.BlockSpec(shape, lambda *_: (0,) * n)


def _proj_kernel(x_ref, wqT_ref, wk_ref, wvT_ref, wp_ref, wg_ref, bg_ref,
                 qT_ref, k_ref, vT_ref, p_ref, gate_ref, *, scale, n_vblk, ta):
    xb = x_ref[...].astype(BF16)
    nt = (((1,), (1,)), ((), ()))
    qT = lax.dot_general(wqT_ref[...], xb, nt, preferred_element_type=F32)
    qT_ref[...] = (qT * scale).astype(BF16)
    k_ref[...] = jnp.dot(xb, wk_ref[...], preferred_element_type=F32).astype(BF16)
    vT = lax.dot_general(wvT_ref[...], xb, nt, preferred_element_type=F32).astype(BF16)
    vd, hr = V_HEAD_DIM, V_HEAD_DIM + DENOM_ROWS
    ones_rows = jnp.where(lax.broadcasted_iota(jnp.int32, (DENOM_ROWS, ta), 0) == 0, 1.0, 0.0).astype(BF16)
    for n in range(n_vblk):
        for h in range(N_HEADS):
            vT_ref[n, h * hr:h * hr + vd, :] = vT[h * vd:(h + 1) * vd, n * ta:(n + 1) * ta]
            vT_ref[n, h * hr + vd:(h + 1) * hr, :] = ones_rows
    p_ref[...] = jnp.dot(xb, wp_ref[...], preferred_element_type=F32)
    g = jnp.dot(xb, wg_ref[...], preferred_element_type=F32) + bg_ref[...]
    gate_ref[...] = jax.nn.sigmoid(g).astype(BF16)


def _project(x2, wqT, wk, wvT, wp, wg, bg, *, tm, ta, scale):
    T, D = x2.shape
    wqk, wv, wpool, wgate = wqT.shape[0], wvT.shape[0], wp.shape[1], wg.shape[1]
    n_vblk = tm // ta
    wvx = N_HEADS * (V_HEAD_DIM + DENOM_ROWS)
    kern = functools.partial(_proj_kernel, scale=scale, n_vblk=n_vblk, ta=ta)
    return pl.pallas_call(
        kern,
        grid=(T // tm,),
        in_specs=[
            pl.BlockSpec((tm, D), lambda i: (i, 0)),
            _full(wqT.shape), _full(wk.shape), _full(wvT.shape), _full(wp.shape), _full(wg.shape),
            _full(bg.shape),
        ],
        out_specs=[
            pl.BlockSpec((wqk, tm), lambda i: (0, i)),
            pl.BlockSpec((tm, wqk), lambda i: (i, 0)),
            pl.BlockSpec((n_vblk, wvx, ta), lambda i: (i, 0, 0)),
            pl.BlockSpec((tm, wpool), lambda i: (i, 0)),
            pl.BlockSpec((tm, wgate), lambda i: (i, 0)),
        ],
        out_shape=[
            jax.ShapeDtypeStruct((wqk, T), BF16),
            jax.ShapeDtypeStruct((T, wqk), BF16),
            jax.ShapeDtypeStruct((T // ta, wvx, ta), BF16),
            jax.ShapeDtypeStruct((T, wpool), F32),
            jax.ShapeDtypeStruct((T, wgate), BF16),
        ],
        compiler_params=_cparams(("arbitrary",)),
        name="input_projection",
    )(x2, wqT, wk, wvT, wp, wg, bg)


def _attn_kernel(slopes_ref, lamp_ref, g_ref, qT_ref, k_ref, vT_ref, o_ref,
                 m_ref, acc_ref, al_ref, base_ref, s0_ref, s1_ref, cm0_ref, cm1_ref, p0_ref, p1_ref,
                 *, tq, tk, lam_init):
    h = pl.program_id(1)
    i = pl.program_id(2)
    slope = slopes_ref[h]
    d = HEAD_DIM
    nrep = 2 * tq // LANES
    qT = qT_ref[...]
    z = jnp.zeros((d, tq), BF16)
    qq = jnp.concatenate([jnp.concatenate([qT[:d], z], axis=0),
                          jnp.concatenate([z, qT[d:]], axis=0)], axis=1)
    m_ref[...] = jnp.full(m_ref.shape, -jnp.inf, F32)
    acc_ref[...] = jnp.zeros(acc_ref.shape, F32)
    al_ref[...] = jnp.ones(al_ref.shape, F32)
    p1_ref[...] = jnp.zeros(p1_ref.shape, BF16)
    base_ref[...] = lax.broadcasted_iota(jnp.int32, (tk, LANES), 0).astype(F32) * slope

    def scores(j, s_ref, cm_ref):
        kb = k_ref[pl.ds(pl.multiple_of(j * tk, tk), tk), :]
        t = jnp.dot(kb, qq, preferred_element_type=F32) + jnp.concatenate([base_ref[...]] * nrep, axis=1)
        s_ref[...] = t
        cm_ref[...] = jnp.max(t, axis=0, keepdims=True)

    def softmax(j, s_ref, cm_ref, p_ref, masked):
        off = (j * tk - i * tq).astype(F32) * slope
        t = s_ref[...]
        if masked:
            key = j * tk + lax.broadcasted_iota(jnp.int32, (tk, 2 * tq), 0)
            col = lax.broadcasted_iota(jnp.int32, (tk, 2 * tq), 1)
            qry = i * tq + jnp.where(col >= tq, col - tq, col)
            t = jnp.where(key > qry, -jnp.inf, t)
            cmax = jnp.max(t, axis=0, keepdims=True)
        else:
            cmax = cm_ref[...]
        m_old = m_ref[...]
        m_new = jnp.maximum(m_old, cmax + off)
        alpha = jnp.exp2(m_old - m_new)
        pr = jnp.exp2(t + (off - m_new))
        p_ref[...] = pr.astype(BF16)
        m_ref[...] = m_new
        return alpha

    def values(j, p_ref):
        vb = vT_ref[jnp.maximum(j, 0)]
        acc_ref[...] = al_ref[...] * acc_ref[...] + jnp.dot(vb, p_ref[...],
                                                             preferred_element_type=F32)

    def half(j, cur, nxt, p_cur, p_prv, masked, prefetch):
        if prefetch:
            scores(j + 1, *nxt)
        alpha = softmax(j, *cur, p_cur, masked)
        values(j - 1, p_prv)
        al_ref[...] = alpha

    def pair(t, masked, last):
        j0 = 2 * t
        half(j0, buf0, buf1, p0_ref, p1_ref, masked, True)
        half(j0 + 1, buf1, buf0, p1_ref, p0_ref, masked, not last)

    buf0, buf1 = (s0_ref, cm0_ref), (s1_ref, cm1_ref)
    scores(0, *buf0)

    def body(t, carry):
        for u in range(PAIRS_PER_TRIP):
            pair(PAIRS_PER_TRIP * t + u, False, False)
        return carry

    npairs = (i * tq) // (2 * tk)
    ntrips = npairs // PAIRS_PER_TRIP
    lax.fori_loop(0, ntrips, body, 0)
    done = ntrips * PAIRS_PER_TRIP
    step = PAIRS_PER_TRIP // 2
    while step:
        take = ((npairs - done) & step) != 0

        @pl.when(take)
        def _(done=done, step=step):
            for u in range(step):
                pair(done + u, False, False)

        done = done + jnp.where(take, step, 0)
        step //= 2

    pair(npairs, True, True)
    values(2 * npairs + 1, p1_ref)

    lamp = lamp_ref[...]
    lam = (jnp.exp(jnp.sum(lamp[0:1] * lamp[1:2], axis=1, keepdims=True))
           - jnp.exp(jnp.sum(lamp[2:3] * lamp[3:4], axis=1, keepdims=True)) + lam_init)
    acc = acc_ref[0:2 * d, :]
    l = acc_ref[2 * d:2 * d + 1, :]
    oT = acc[:, :tq] / l[:, :tq] - lam * (acc[:, tq:] / l[:, tq:])
    ms = jnp.mean(oT * oT, axis=0, keepdims=True)
    oT = oT * lax.rsqrt(ms + LN_EPS)
    o = oT.T * g_ref[...] * (1.0 - lam_init)
    o_ref[...] = o.astype(BF16)


def _attention(slopes, lamp, subln_g, qT, k, vTb, *, B, S, tq, tk, lam_init):
    T = B * S
    nq = S // tq
    nk = S // tk
    assert tq in (tk, 2 * tk) and nk % 2 == 0
    vd = V_HEAD_DIM
    kern = functools.partial(_attn_kernel, tq=tq, tk=tk, lam_init=lam_init)
    return pl.pallas_call(
        kern,
        grid=(B, N_HEADS, nq),
        in_specs=[
            pl.BlockSpec(memory_space=pltpu.SMEM),
            _full(lamp.shape),
            _full(subln_g.shape),
            pl.BlockSpec((vd, tq), lambda b, h, i: (h, b * nq + i)),
            pl.BlockSpec((S, vd), lambda b, h, i: (b, h)),
            pl.BlockSpec((nk, vd + DENOM_ROWS, tk), lambda b, h, i: (b, h, 0)),
        ],
        out_specs=pl.BlockSpec((tq, vd), lambda b, h, i: (b * nq + i, h)),
        out_shape=jax.ShapeDtypeStruct((T, N_HEADS * vd), BF16),
        scratch_shapes=[
            pltpu.VMEM((1, 2 * tq), F32),
            pltpu.VMEM((vd + DENOM_ROWS, 2 * tq), F32),
            pltpu.VMEM((1, 2 * tq), F32),
            pltpu.VMEM((tk, LANES), F32),
            pltpu.VMEM((tk, 2 * tq), F32),
            pltpu.VMEM((tk, 2 * tq), F32),
            pltpu.VMEM((1, 2 * tq), F32),
            pltpu.VMEM((1, 2 * tq), F32),
            pltpu.VMEM((tk, 2 * tq), BF16),
            pltpu.VMEM((tk, 2 * tq), BF16),
        ],
        compiler_params=_cparams(("arbitrary", "arbitrary", "arbitrary")),
        name="diff_attention",
    )(slopes, lamp, subln_g, qT, k, vTb)


def _layer_norm_rows(r, g, b):
    mu = jnp.mean(r, axis=-1, keepdims=True)
    dlt = r - mu
    var = jnp.mean(dlt * dlt, axis=-1, keepdims=True)
    return dlt * lax.rsqrt(var + LN_EPS) * g + b


def _merge_kernel(o_ref, p_ref, ph_ref, gate_ref, x_ref, wap_ref, pw_ref, ps_ref, wpp_ref,
                  wout_ref, g1_ref, b1_ref, h1_ref, ext_ref, *, tm, seq, dn_alpha):
    i = pl.program_id(0)
    t0 = (i * tm) % seq
    hl = POOL_HALO
    ext_ref[0:hl, :] = jnp.where(t0 == 0, 0.0, ph_ref[...])
    ext_ref[hl:hl + tm, :] = p_ref[...]
    pos = (t0 + lax.broadcasted_iota(jnp.int32, (tm, POOL_GROUP_DIM), 0)).astype(F32)
    pm = []
    for gi, w in enumerate(POOL_WINDOWS):
        sl = slice(gi * POOL_GROUP_DIM, (gi + 1) * POOL_GROUP_DIM)
        cur = ext_ref[hl:hl + tm, sl]
        win = cur
        for back in range(1, w):
            win = win + ext_ref[hl - back:hl - back + tm, sl]
        cnt = jnp.minimum(float(w), pos + 1.0)
        pooled = win / cnt - cur
        pm.append(jnp.dot(pooled.astype(BF16), pw_ref[gi], preferred_element_type=F32))
    pm = jnp.concatenate(pm, axis=1) * ps_ref[...]
    pool_branch = jnp.dot(pm.astype(BF16), wpp_ref[...], preferred_element_type=F32)
    attn_branch = jnp.dot(o_ref[...], wap_ref[...], preferred_element_type=F32)
    dm = attn_branch.shape[1]
    gates = gate_ref[...].astype(F32)
    merged = gates[:, :dm] * attn_branch + gates[:, dm:] * pool_branch
    mix = jnp.dot(merged.astype(BF16), wout_ref[...], preferred_element_type=F32)
    h1_ref[...] = _layer_norm_rows(dn_alpha * x_ref[...] + mix, g1_ref[...], b1_ref[...])


def _merge(o, p, gates, x2, wap, pw, ps, wpp, wout, g1, b1, *, tm, seq, dn_alpha):
    T, D = x2.shape
    pwid = p.shape[1]
    hb = tm // POOL_HALO
    kern = functools.partial(_merge_kernel, tm=tm, seq=seq, dn_alpha=dn_alpha)
    return pl.pallas_call(
        kern,
        grid=(T // tm,),
        in_specs=[
            pl.BlockSpec((tm, o.shape[1]), lambda i: (i, 0)),
            pl.BlockSpec((tm, pwid), lambda i: (i, 0)),
            pl.BlockSpec((POOL_HALO, pwid), lambda i: (jnp.maximum(i * hb - 1, 0), 0)),
            pl.BlockSpec((tm, gates.shape[1]), lambda i: (i, 0)),
            pl.BlockSpec((tm, D), lambda i: (i, 0)),
            _full(wap.shape), _full(pw.shape), _full(ps.shape), _full(wpp.shape), _full(wout.shape),
            _full(g1.shape), _full(b1.shape),
        ],
        out_specs=pl.BlockSpec((tm, D), lambda i: (i, 0)),
        out_shape=jax.ShapeDtypeStruct((T, D), F32),
        scratch_shapes=[pltpu.VMEM((POOL_HALO + tm, pwid), F32)],
        compiler_params=_cparams(("arbitrary",)),
        name="merge_ln1",
    )(o, p, p, gates, x2, wap, pw, ps, wpp, wout, g1, b1)


def _top_rows(s, k, payload=None):
    n, t = s.shape
    ng = n // SUBLANES
    sub = lax.broadcasted_iota(jnp.int32, (SUBLANES, t), 0)
    tiles = [s[g * SUBLANES:(g + 1) * SUBLANES] for g in range(ng)]
    rows = [sub + g * SUBLANES for g in range(ng)]
    vals, picks = [], []
    for _ in range(k):
        best, first = tiles[0], jnp.zeros((SUBLANES, t), jnp.int32)
        for g in range(1, ng):
            first = jnp.where(tiles[g] > best, g, first)
            best = jnp.maximum(best, tiles[g])
        m = jnp.max(best, axis=0, keepdims=True)
        am = jnp.min(jnp.where(best == m, first * SUBLANES + sub, n), axis=0, keepdims=True)
        hits = [rows[g] == am for g in range(ng)]
        vals.append(m)
        if payload is None:
            picks.append(am)
        else:
            found = [jnp.where(hits[g], payload[g * SUBLANES:(g + 1) * SUBLANES], 0) for g in range(ng)]
            picks.append(jnp.sum(functools.reduce(jnp.add, found), axis=0, keepdims=True))
        tiles = [jnp.where(hits[g], -jnp.inf, tiles[g]) for g in range(ng)]
    return jnp.concatenate(vals, axis=0), jnp.concatenate(picks, axis=0)


def _packed_candidates(s1, i1, s2, i2, sub):
    kk = s1.shape[0]
    segs, start = [], 0
    for a in range(kk):
        segs.append((a, start, kk // (a + 1)))
        start += kk // (a + 1)
    total = start
    cand, eid = [], []
    for lo in range(0, total, SUBLANES):
        cv = ev = None
        for a, st, nb in segs:
            if st + nb <= lo or st >= lo + SUBLANES:
                continue
            if st <= lo and nb > SUBLANES:
                src_s, src_i = s2[lo - st:lo - st + SUBLANES], i2[lo - st:lo - st + SUBLANES]
            else:
                amt = (st - lo) % SUBLANES
                src_s = pltpu.roll(s2[:SUBLANES], amt, axis=0) if amt else s2[:SUBLANES]
                src_i = pltpu.roll(i2[:SUBLANES], amt, axis=0) if amt else i2[:SUBLANES]
            val = s1[a:a + 1] + src_s
            idv = i1[a:a + 1] * PEER_KEYS + src_i
            p0 = max(st - lo, 0)
            cv = val if p0 == 0 else jnp.where(sub >= p0, val, cv)
            ev = idv if p0 == 0 else jnp.where(sub >= p0, idv, ev)
        if total - lo < SUBLANES:
            cv = jnp.where(sub >= total - lo, -jnp.inf, cv)
        cand.append(cv)
        eid.append(ev)
    return jnp.concatenate(cand, axis=0), jnp.concatenate(eid, axis=0)


def _route_kernel(y_ref, wq_ref, sk_ref, idx_ref, gate_ref):
    yb = y_ref[...].astype(BF16)
    tm = yb.shape[0]
    nt = (((1,), (1,)), ((), ()))
    kk = PEER_TOPK
    sub = lax.broadcasted_iota(jnp.int32, (SUBLANES, tm), 0)

    def head(h, carry):
        q = jnp.dot(yb, wq_ref[h], preferred_element_type=F32).astype(BF16)
        top = []
        for part in range(2):
            qp = q[:, part * PEER_HALF:(part + 1) * PEER_HALF]
            sT = lax.dot_general(sk_ref[2 * h + part], qp, nt, preferred_element_type=F32)
            top.append(_top_rows(sT, kk))
        (s1, i1), (s2, i2) = top
        cand, eid = _packed_candidates(s1, i1, s2, i2, sub)
        sc, ids = _top_rows(cand, kk, payload=eid)
        e = jnp.exp(sc - sc[0:1])
        gate = e / jnp.sum(e, axis=0, keepdims=True)
        r0 = pl.multiple_of(h * kk, kk)
        idx_ref[pl.ds(r0, kk), :] = ids
        gate_ref[pl.ds(r0, kk), :] = gate
        return carry

    lax.fori_loop(0, PEER_HEADS, head, 0)


def _route(h1, wq3, sk, *, tm):
    T, D = h1.shape
    return pl.pallas_call(
        _route_kernel,
        grid=(T // tm,),
        in_specs=[pl.BlockSpec((tm, D), lambda i: (i, 0)), _full(wq3.shape), _full(sk.shape)],
        out_specs=[pl.BlockSpec((PEER_SLOTS, tm), lambda i: (0, i)),
                   pl.BlockSpec((PEER_SLOTS, tm), lambda i: (0, i))],
        out_shape=[jax.ShapeDtypeStruct((PEER_SLOTS, T), jnp.int32),
                   jax.ShapeDtypeStruct((PEER_SLOTS, T), F32)],
        compiler_params=_cparams(("arbitrary",)),
        name="peer_route",
    )(h1, wq3, sk)


def _pack_table(t):
    n, dd = t.shape
    bits = lax.bitcast_convert_type(t, jnp.uint32)
    b16 = (bits + jnp.uint32(0x7FFF) + ((bits >> 16) & jnp.uint32(1))) >> 16
    return (b16[:, :dd // 2] | (b16[:, dd // 2:] << 16)).reshape(n * (dd // (2 * LANES)), LANES)


def _unpack_words(w):
    lo = lax.bitcast_convert_type(w << 16, F32)
    hi = lax.bitcast_convert_type(w & jnp.uint32(0xFFFF0000), F32)
    return lo, hi


def _gather_rows(idx_ref, tab_ref, rs, c, dst):
    irow = idx_ref.at[c]
    for e in range(PEER_SLOTS):
        r0 = pl.multiple_of(irow[e], rs)
        dst[pl.ds(e, rs, stride=CHUNK_STRIDE), :] = tab_ref[pl.ds(r0, rs), :]


def _token_pipeline(tb, gather, fold, ca_ref, cb_ref):
    gather(0, ca_ref)

    def pair(kp, carry):
        c0 = 2 * kp
        gather(c0 + 1, cb_ref)
        carry = fold(c0, ca_ref, carry)
        gather(jnp.minimum(c0 + 2, tb - 1), ca_ref)
        return fold(c0 + 1, cb_ref, carry)

    return pair


def _score_kernel(idx_ref, y_ref, gate_ref, tab_ref, w_ref, ca_ref, cb_ref, *, tb):
    nsl = PEER_SLOTS
    rs = y_ref.shape[1] // 2
    st = CHUNK_STRIDE
    half = rs * LANES
    lane = lax.broadcasted_iota(jnp.int32, (nsl, tb), 1)

    def place(c, tot):
        w_ref[...] = jnp.where(lane == c, jnp.sum(tot, axis=1, keepdims=True), w_ref[...])

    def fold(c, src, pend):
        place(c - 1, pend)
        tot = None
        yv = y_ref[c]
        for s in range(rs):
            lo, hi = _unpack_words(src[s * st:s * st + nsl, :])
            part = lo * yv[s:s + 1] + hi * yv[rs + s:rs + s + 1]
            tot = part if tot is None else tot + part
        return tot

    w_ref[...] = jnp.zeros(w_ref.shape, F32)
    gather = functools.partial(_gather_rows, idx_ref, tab_ref, rs)
    pair = _token_pipeline(tb, gather, fold, ca_ref, cb_ref)
    pend = lax.fori_loop(0, tb // 2, pair, jnp.zeros((nsl, LANES), F32))
    place(tb - 1, pend)
    act = w_ref[...]
    gelu = 0.5 * act * (1.0 + lax.erf(act * (2.0 ** -0.5)))
    w_ref[...] = gate_ref[...] * gelu


def _table_spec(tab):
    return pl.BlockSpec(tab.shape, lambda i: (0, 0), pipeline_mode=pl.Buffered(1))


def _chunk_scratch(rs):
    return pltpu.VMEM((CHUNK_STRIDE * rs, LANES), jnp.uint32)


def _expert_scores(idx, y3, gateT, utab, *, tb):
    T = y3.shape[0]
    kern = functools.partial(_score_kernel, tb=tb)
    return pl.pallas_call(
        kern,
        grid=(T // tb,),
        in_specs=[
            pl.BlockSpec((tb, PEER_SLOTS), lambda i: (i, 0), memory_space=pltpu.SMEM),
            pl.BlockSpec((tb,) + y3.shape[1:], lambda i: (i, 0, 0)),
            pl.BlockSpec((PEER_SLOTS, tb), lambda i: (0, i)),
            _table_spec(utab),
        ],
        out_specs=pl.BlockSpec((PEER_SLOTS, tb), lambda i: (0, i)),
        out_shape=jax.ShapeDtypeStruct((PEER_SLOTS, T), F32),
        scratch_shapes=[_chunk_scratch(y3.shape[1] // 2)] * 2,
        compiler_params=_cparams(("arbitrary",)),
        name="peer_scores",
    )(idx, y3, gateT, utab)


def _mix_kernel(idx_ref, w_ref, tab_ref, f_ref, ca_ref, cb_ref, *, tb):
    nsl = PEER_SLOTS
    rs = f_ref.shape[1] // 2
    st = CHUNK_STRIDE
    half = rs * LANES
    lane = lax.broadcasted_iota(jnp.int32, (nsl, tb), 1)

    def fold(c, src, carry):
        wcol = jnp.sum(jnp.where(lane == c, w_ref[...], 0.0), axis=1, keepdims=True)
        wb = jnp.broadcast_to(wcol, (nsl, LANES))
        los, his = [], []
        for s in range(rs):
            lo, hi = _unpack_words(src[s * st:s * st + nsl, :])
            los.append(jnp.sum(lo * wb, axis=0, keepdims=True))
            his.append(jnp.sum(hi * wb, axis=0, keepdims=True))
        f_ref[c] = jnp.concatenate(los + his, axis=0)
        return carry

    gather = functools.partial(_gather_rows, idx_ref, tab_ref, rs)
    pair = _token_pipeline(tb, gather, fold, ca_ref, cb_ref)
    lax.fori_loop(0, tb // 2, pair, 0)


def _expert_mix(idx, wT, vtab, *, tb, rs):
    T = idx.shape[0]
    rows = 2 * rs
    kern = functools.partial(_mix_kernel, tb=tb)
    return pl.pallas_call(
        kern,
        grid=(T // tb,),
        in_specs=[
            pl.BlockSpec((tb, PEER_SLOTS), lambda i: (i, 0), memory_space=pltpu.SMEM),
            pl.BlockSpec((PEER_SLOTS, tb), lambda i: (0, i)),
            _table_spec(vtab),
        ],
        out_specs=pl.BlockSpec((tb, rows, LANES), lambda i: (i, 0, 0)),
        out_shape=jax.ShapeDtypeStruct((T, rows, LANES), F32),
        scratch_shapes=[_chunk_scratch(rs)] * 2,
        compiler_params=_cparams(("arbitrary",)),
        name="peer_mix",
    )(idx, wT, vtab)


def _ln2_kernel(h_ref, f_ref, g_ref, b_ref, o_ref, *, dn_alpha):
    o_ref[...] = _layer_norm_rows(dn_alpha * h_ref[...] + f_ref[...], g_ref[...], b_ref[...])


def _residual_ln(h1, ffn, g, b, *, tm, dn_alpha):
    T, D = h1.shape
    return pl.pallas_call(
        functools.partial(_ln2_kernel, dn_alpha=dn_alpha),
        grid=(T // tm,),
        in_specs=[pl.BlockSpec((tm, D), lambda i: (i, 0)), pl.BlockSpec((tm, D), lambda i: (i, 0)),
                  _full(g.shape), _full(b.shape)],
        out_specs=pl.BlockSpec((tm, D), lambda i: (i, 0)),
        out_shape=jax.ShapeDtypeStruct((T, D), F32),
        compiler_params=_cparams(("arbitrary",)),
        name="residual_ln2",
    )(h1, ffn, g, b)


def _layer(h2d, B, S, depth, lam_init, w_in, b_gate, lq1, lk1, lq2, lk2, subln_g, w_attn_proj, pool_w,
           pool_scale, w_pool_proj, w_out, ln1_g, ln1_b, peer_wq, peer_subkeys, peer_u, peer_v,
           ln2_g, ln2_b):
    T, D = h2d.shape
    dn_alpha = (2.0 * depth) ** 0.25
    wqk = 2 * N_HEADS * HEAD_DIM
    wv = N_HEADS * V_HEAD_DIM
    wpool = len(POOL_WINDOWS) * POOL_GROUP_DIM
    tm = min(PROJ_ROWS, S)
    ta = min(ATTN_BLOCK, S // 2)
    c0, c1, c2, c3 = wqk, 2 * wqk, 2 * wqk + wv, 2 * wqk + wv + wpool
    wb = w_in.astype(BF16)
    row = lambda a: a.reshape(1, -1).astype(F32)

    qT, k, vTb, p, gates = _project(
        h2d, wb[:, :c0].T, wb[:, c0:c1], wb[:, c1:c2].T, wb[:, c2:c3], wb[:, c3:], row(b_gate),
        tm=tm, ta=ta, scale=HEAD_DIM ** -0.5 * LOG2E)

    slopes = jnp.asarray(np.array([2.0 ** (-8.0 * (h + 1) / N_HEADS) * LOG2E for h in range(N_HEADS)],
                                  dtype=np.float32))
    lamp = jnp.stack([lq1, lk1, lq2, lk2]).astype(F32)
    o = _attention(slopes, lamp, row(subln_g), qT, k, vTb, B=B, S=S, tq=ta, tk=ta, lam_init=lam_init)

    h1 = _merge(o, p, gates, h2d, w_attn_proj.astype(BF16), pool_w.astype(BF16), row(pool_scale),
                w_pool_proj.astype(BF16), w_out.astype(BF16), row(ln1_g), row(ln1_b),
                tm=tm, seq=S, dn_alpha=dn_alpha)

    wq3 = peer_wq.astype(BF16).reshape(D, PEER_HEADS, 2 * PEER_HALF).transpose(1, 0, 2)
    sk = peer_subkeys.astype(BF16).reshape(PEER_HEADS * 2, PEER_KEYS, PEER_HALF)
    idxT, gateT = _route(h1, wq3, sk, tm=min(ROUTE_ROWS, T))

    tb = min(PEER_ROWS, T)
    rs = D // (2 * LANES)
    idx = idxT.T * rs
    wT = _expert_scores(idx, h1.reshape(T, D // LANES, LANES), gateT, _pack_table(peer_u), tb=tb)
    ffn = _expert_mix(idx, wT, _pack_table(peer_v), tb=tb, rs=rs)
    return _residual_ln(h1, ffn.reshape(T, D), row(ln2_g), row(ln2_b), tm=tm, dn_alpha=dn_alpha)


def kernel(x, w_in, b_gate, lambda_q1, lambda_k1, lambda_q2, lambda_k2, subln_g, w_attn_proj, pool_w,
           pool_scale, w_pool_proj, w_out, ln1_g, ln1_b, peer_wq, peer_subkeys, peer_u, peer_v,
           ln2_g, ln2_b):
    B, S, D = x.shape
    depth = w_in.shape[0]
    h = x.reshape(B * S, D)
    for l in range(depth):
        lam_init = 0.8 - 0.6 * math.exp(-0.3 * l)
        h = _layer(h, B, S, depth, lam_init, w_in[l], b_gate[l], lambda_q1[l], lambda_k1[l],
                   lambda_q2[l], lambda_k2[l], subln_g[l], w_attn_proj[l], pool_w[l], pool_scale[l],
                   w_pool_proj[l], w_out[l], ln1_g[l], ln1_b[l], peer_wq[l], peer_subkeys[l],
                   peer_u[l], peer_v[l], ln2_g[l], ln2_b[l])
    return h.reshape(B, S, D)
```

```python
import functools
import math

import jax
import jax.numpy as jnp
import numpy as np
from jax import lax
from jax.experimental import pallas as pl
from jax.experimental.pallas import tpu as pltpu

F32 = jnp.float32
BF16 = jnp.bfloat16

N_HEADS = 8
HEAD_DIM = 64
V_HEAD_DIM = 2 * HEAD_DIM
POOL_WINDOWS = (2, 4, 8, 16)
POOL_GROUP_DIM = 128
POOL_HALO = 16
PEER_HEADS = 8
PEER_KEYS = 128
PEER_HALF = 128
PEER_TOPK = 16
PEER_SLOTS = PEER_HEADS * PEER_TOPK
LN_EPS = 1e-5
LOG2E = math.log2(math.e)

LANES = 128
SUBLANES = 8
VMEM_LIMIT_BYTES = 56 * 1024 * 1024
PROJ_ROWS = 512
ATTN_BLOCK = 256
ROUTE_ROWS = 2048
PEER_ROWS = 128
CHUNK_STRIDE = PEER_SLOTS + 1
DENOM_ROWS = 16
PAIRS_PER_TRIP = 8


def _cparams(sem):
    return pltpu.CompilerParams(dimension_semantics=sem, vmem_limit_bytes=VMEM_LIMIT_BYTES)


def _full(shape):
    n = len(shape)
    return pl.BlockSpec(shape, lambda *_: (0,) * n)


def _proj_kernel(x_ref, wqT_ref, wk_ref, wvT_ref, wp_ref, wg_ref, bg_ref,
                 qT_ref, k_ref, vT_ref, p_ref, gate_ref, *, scale, n_vblk, ta):
    xb = x_ref[...].astype(BF16)
    nt = (((1,), (1,)), ((), ()))
    qT = lax.dot_general(wqT_ref[...], xb, nt, preferred_element_type=F32)
    qT_ref[...] = (qT * scale).astype(BF16)
    k_ref[...] = jnp.dot(xb, wk_ref[...], preferred_element_type=F32).astype(BF16)
    vT = lax.dot_general(wvT_ref[...], xb, nt, preferred_element_type=F32).astype(BF16)
    vd, hr = V_HEAD_DIM, V_HEAD_DIM + DENOM_ROWS
    ones_rows = jnp.where(lax.broadcasted_iota(jnp.int32, (DENOM_ROWS, ta), 0) == 0, 1.0, 0.0).astype(BF16)
    for n in range(n_vblk):
        for h in range(N_HEADS):
            vT_ref[n, h * hr:h * hr + vd, :] = vT[h * vd:(h + 1) * vd, n * ta:(n + 1) * ta]
            vT_ref[n, h * hr + vd:(h + 1) * hr, :] = ones_rows
    p_ref[...] = jnp.dot(xb, wp_ref[...], preferred_element_type=F32)
    g = jnp.dot(xb, wg_ref[...], preferred_element_type=F32) + bg_ref[...]
    gate_ref[...] = jax.nn.sigmoid(g).astype(BF16)


def _project(x2, wqT, wk, wvT, wp, wg, bg, *, tm, ta, scale):
    T, D = x2.shape
    wqk, wv, wpool, wgate = wqT.shape[0], wvT.shape[0], wp.shape[1], wg.shape[1]
    n_vblk = tm // ta
    wvx = N_HEADS * (V_HEAD_DIM + DENOM_ROWS)
    kern = functools.partial(_proj_kernel, scale=scale, n_vblk=n_vblk, ta=ta)
    return pl.pallas_call(
        kern,
        grid=(T // tm,),
        in_specs=[
            pl.BlockSpec((tm, D), lambda i: (i, 0)),
            _full(wqT.shape), _full(wk.shape), _full(wvT.shape), _full(wp.shape), _full(wg.shape),
            _full(bg.shape),
        ],
        out_specs=[
            pl.BlockSpec((wqk, tm), lambda i: (0, i)),
            pl.BlockSpec((tm, wqk), lambda i: (i, 0)),
            pl.BlockSpec((n_vblk, wvx, ta), lambda i: (i, 0, 0)),
            pl.BlockSpec((tm, wpool), lambda i: (i, 0)),
            pl.BlockSpec((tm, wgate), lambda i: (i, 0)),
        ],
        out_shape=[
            jax.ShapeDtypeStruct((wqk, T), BF16),
            jax.ShapeDtypeStruct((T, wqk), BF16),
            jax.ShapeDtypeStruct((T // ta, wvx, ta), BF16),
            jax.ShapeDtypeStruct((T, wpool), F32),
            jax.ShapeDtypeStruct((T, wgate), BF16),
        ],
        compiler_params=_cparams(("arbitrary",)),
        name="input_projection",
    )(x2, wqT, wk, wvT, wp, wg, bg)


def _attn_kernel(slopes_ref, lamp_ref, g_ref, qT_ref, k_ref, vT_ref, o_ref,
                 m_ref, acc_ref, al_ref, base_ref, s0_ref, s1_ref, cm0_ref, cm1_ref, p0_ref, p1_ref,
                 *, tq, tk, lam_init):
    h = pl.program_id(1)
    i = pl.program_id(2)
    slope = slopes_ref[h]
    d = HEAD_DIM
    nrep = 2 * tq // LANES
    qT = qT_ref[...]
    z = jnp.zeros((d, tq), BF16)
    qq = jnp.concatenate([jnp.concatenate([qT[:d], z], axis=0),
                          jnp.concatenate([z, qT[d:]], axis=0)], axis=1)
    m_ref[...] = jnp.full(m_ref.shape, -jnp.inf, F32)
    acc_ref[...] = jnp.zeros(acc_ref.shape, F32)
    al_ref[...] = jnp.ones(al_ref.shape, F32)
    p1_ref[...] = jnp.zeros(p1_ref.shape, BF16)
    base_ref[...] = lax.broadcasted_iota(jnp.int32, (tk, LANES), 0).astype(F32) * slope

    def scores(j, s_ref, cm_ref):
        kb = k_ref[pl.ds(pl.multiple_of(j * tk, tk), tk), :]
        t = jnp.dot(kb, qq, preferred_element_type=F32) + jnp.concatenate([base_ref[...]] * nrep, axis=1)
        s_ref[...] = t
        cm_ref[...] = jnp.max(t, axis=0, keepdims=True)

    def softmax(j, s_ref, cm_ref, p_ref, masked):
        off = (j * tk - i * tq).astype(F32) * slope
        t = s_ref[...]
        if masked:
            key = j * tk + lax.broadcasted_iota(jnp.int32, (tk, 2 * tq), 0)
            col = lax.broadcasted_iota(jnp.int32, (tk, 2 * tq), 1)
            qry = i * tq + jnp.where(col >= tq, col - tq, col)
            t = jnp.where(key > qry, -jnp.inf, t)
            cmax = jnp.max(t, axis=0, keepdims=True)
        else:
            cmax = cm_ref[...]
        m_old = m_ref[...]
        m_new = jnp.maximum(m_old, cmax + off)
        alpha = jnp.exp2(m_old - m_new)
        pr = jnp.exp2(t + (off - m_new))
        p_ref[...] = pr.astype(BF16)
        m_ref[...] = m_new
        return alpha

    def values(j, p_ref):
        vb = vT_ref[jnp.maximum(j, 0)]
        acc_ref[...] = al_ref[...] * acc_ref[...] + jnp.dot(vb, p_ref[...],
                                                             preferred_element_type=F32)

    def half(j, cur, nxt, p_cur, p_prv, masked, prefetch):
        if prefetch:
            scores(j + 1, *nxt)
        alpha = softmax(j, *cur, p_cur, masked)
        values(j - 1, p_prv)
        al_ref[...] = alpha

    def pair(t, masked, last):
        j0 = 2 * t
        half(j0, buf0, buf1, p0_ref, p1_ref, masked, True)
        half(j0 + 1, buf1, buf0, p1_ref, p0_ref, masked, not last)

    buf0, buf1 = (s0_ref, cm0_ref), (s1_ref, cm1_ref)
    scores(0, *buf0)

    def body(t, carry):
        for u in range(PAIRS_PER_TRIP):
            pair(PAIRS_PER_TRIP * t + u, False, False)
        return carry

    npairs = (i * tq) // (2 * tk)
    ntrips = npairs // PAIRS_PER_TRIP
    lax.fori_loop(0, ntrips, body, 0)
    done = ntrips * PAIRS_PER_TRIP
    step = PAIRS_PER_TRIP // 2
    while step:
        take = ((npairs - done) & step) != 0

        @pl.when(take)
        def _(done=done, step=step):
            for u in range(step):
                pair(done + u, False, False)

        done = done + jnp.where(take, step, 0)
        step //= 2

    pair(npairs, True, True)
    values(2 * npairs + 1, p1_ref)

    lamp = lamp_ref[...]
    lam = (jnp.exp(jnp.sum(lamp[0:1] * lamp[1:2], axis=1, keepdims=True))
           - jnp.exp(jnp.sum(lamp[2:3] * lamp[3:4], axis=1, keepdims=True)) + lam_init)
    acc = acc_ref[0:2 * d, :]
    l = acc_ref[2 * d:2 * d + 1, :]
    oT = acc[:, :tq] / l[:, :tq] - lam * (acc[:, tq:] / l[:, tq:])
    ms = jnp.mean(oT * oT, axis=0, keepdims=True)
    oT = oT * lax.rsqrt(ms + LN_EPS)
    o = oT.T * g_ref[...] * (1.0 - lam_init)
    o_ref[...] = o.astype(BF16)


def _attention(slopes, lamp, subln_g, qT, k, vTb, *, B, S, tq, tk, lam_init):
    T = B * S
    nq = S // tq
    nk = S // tk
    assert tq in (tk, 2 * tk) and nk % 2 == 0
    vd = V_HEAD_DIM
    kern = functools.partial(_attn_kernel, tq=tq, tk=tk, lam_init=lam_init)
    return pl.pallas_call(
        kern,
        grid=(B, N_HEADS, nq),
        in_specs=[
            pl.BlockSpec(memory_space=pltpu.SMEM),
            _full(lamp.shape),
            _full(subln_g.shape),
            pl.BlockSpec((vd, tq), lambda b, h, i: (h, b * nq + i)),
            pl.BlockSpec((S, vd), lambda b, h, i: (b, h)),
            pl.BlockSpec((nk, vd + DENOM_ROWS, tk), lambda b, h, i: (b, h, 0)),
        ],
        out_specs=pl.BlockSpec((tq, vd), lambda b, h, i: (b * nq + i, h)),
        out_shape=jax.ShapeDtypeStruct((T, N_HEADS * vd), BF16),
        scratch_shapes=[
            pltpu.VMEM((1, 2 * tq), F32),
            pltpu.VMEM((vd + DENOM_ROWS, 2 * tq), F32),
            pltpu.VMEM((1, 2 * tq), F32),
            pltpu.VMEM((tk, LANES), F32),
            pltpu.VMEM((tk, 2 * tq), F32),
            pltpu.VMEM((tk, 2 * tq), F32),
            pltpu.VMEM((1, 2 * tq), F32),
            pltpu.VMEM((1, 2 * tq), F32),
            pltpu.VMEM((tk, 2 * tq), BF16),
            pltpu.VMEM((tk, 2 * tq), BF16),
        ],
        compiler_params=_cparams(("arbitrary", "arbitrary", "arbitrary")),
        name="diff_attention",
    )(slopes, lamp, subln_g, qT, k, vTb)


def _layer_norm_rows(r, g, b):
    mu = jnp.mean(r, axis=-1, keepdims=True)
    dlt = r - mu
    var = jnp.mean(dlt * dlt, axis=-1, keepdims=True)
    return dlt * lax.rsqrt(var + LN_EPS) * g + b


def _merge_kernel(o_ref, p_ref, ph_ref, gate_ref, x_ref, wap_ref, pw_ref, ps_ref, wpp_ref,
                  wout_ref, g1_ref, b1_ref, h1_ref, h3_ref, ext_ref, *, tm, seq, dn_alpha):
    i = pl.program_id(0)
    t0 = (i * tm) % seq
    hl = POOL_HALO
    ext_ref[0:hl, :] = jnp.where(t0 == 0, 0.0, ph_ref[...])
    ext_ref[hl:hl + tm, :] = p_ref[...]
    pos = (t0 + lax.broadcasted_iota(jnp.int32, (tm, POOL_GROUP_DIM), 0)).astype(F32)
    pm = []
    for gi, w in enumerate(POOL_WINDOWS):
        sl = slice(gi * POOL_GROUP_DIM, (gi + 1) * POOL_GROUP_DIM)
        cur = ext_ref[hl:hl + tm, sl]
        win = cur
        for back in range(1, w):
            win = win + ext_ref[hl - back:hl - back + tm, sl]
        cnt = jnp.minimum(float(w), pos + 1.0)
        pooled = win / cnt - cur
        pm.append(jnp.dot(pooled.astype(BF16), pw_ref[gi], preferred_element_type=F32))
    pm = jnp.concatenate(pm, axis=1) * ps_ref[...]
    pool_branch = jnp.dot(pm.astype(BF16), wpp_ref[...], preferred_element_type=F32)
    attn_branch = jnp.dot(o_ref[...], wap_ref[...], preferred_element_type=F32)
    dm = attn_branch.shape[1]
    gates = gate_ref[...].astype(F32)
    merged = gates[:, :dm] * attn_branch + gates[:, dm:] * pool_branch
    mix = jnp.dot(merged.astype(BF16), wout_ref[...], preferred_element_type=F32)
    h1 = _layer_norm_rows(dn_alpha * x_ref[...] + mix, g1_ref[...], b1_ref[...])
    h1_ref[...] = h1
    for s in range(h3_ref.shape[1]):
        h3_ref[:, s, :] = h1[:, s * LANES:(s + 1) * LANES]


def _merge(o, p, gates, x2, wap, pw, ps, wpp, wout, g1, b1, *, tm, seq, dn_alpha):
    T, D = x2.shape
    pwid = p.shape[1]
    hb = tm // POOL_HALO
    kern = functools.partial(_merge_kernel, tm=tm, seq=seq, dn_alpha=dn_alpha)
    return pl.pallas_call(
        kern,
        grid=(T // tm,),
        in_specs=[
            pl.BlockSpec((tm, o.shape[1]), lambda i: (i, 0)),
            pl.BlockSpec((tm, pwid), lambda i: (i, 0)),
            pl.BlockSpec((POOL_HALO, pwid), lambda i: (jnp.maximum(i * hb - 1, 0), 0)),
            pl.BlockSpec((tm, gates.shape[1]), lambda i: (i, 0)),
            pl.BlockSpec((tm, D), lambda i: (i, 0)),
            _full(wap.shape), _full(pw.shape), _full(ps.shape), _full(wpp.shape), _full(wout.shape),
            _full(g1.shape), _full(b1.shape),
        ],
        out_specs=[pl.BlockSpec((tm, D), lambda i: (i, 0)),
                   pl.BlockSpec((tm, D // LANES, LANES), lambda i: (i, 0, 0))],
        out_shape=[jax.ShapeDtypeStruct((T, D), F32),
                   jax.ShapeDtypeStruct((T, D // LANES, LANES), F32)],
        scratch_shapes=[pltpu.VMEM((POOL_HALO + tm, pwid), F32)],
        compiler_params=_cparams(("arbitrary",)),
        name="merge_ln1",
    )(o, p, p, gates, x2, wap, pw, ps, wpp, wout, g1, b1)


def _top_rows(s, k, payload=None):
    n, t = s.shape
    ng = n // SUBLANES
    sub = lax.broadcasted_iota(jnp.int32, (SUBLANES, t), 0)
    tiles = [s[g * SUBLANES:(g + 1) * SUBLANES] for g in range(ng)]
    rows = [sub + g * SUBLANES for g in range(ng)]
    vals, picks = [], []
    for _ in range(k):
        best, first = tiles[0], jnp.zeros((SUBLANES, t), jnp.int32)
        for g in range(1, ng):
            first = jnp.where(tiles[g] > best, g, first)
            best = jnp.maximum(best, tiles[g])
        m = jnp.max(best, axis=0, keepdims=True)
        am = jnp.min(jnp.where(best == m, first * SUBLANES + sub, n), axis=0, keepdims=True)
        hits = [rows[g] == am for g in range(ng)]
        vals.append(m)
        if payload is None:
            picks.append(am)
        else:
            found = [jnp.where(hits[g], payload[g * SUBLANES:(g + 1) * SUBLANES], 0) for g in range(ng)]
            picks.append(jnp.sum(functools.reduce(jnp.add, found), axis=0, keepdims=True))
        tiles = [jnp.where(hits[g], -jnp.inf, tiles[g]) for g in range(ng)]
    return jnp.concatenate(vals, axis=0), jnp.concatenate(picks, axis=0)


def _packed_candidates(s1, i1, s2, i2, sub):
    kk = s1.shape[0]
    segs, start = [], 0
    for a in range(kk):
        segs.append((a, start, kk // (a + 1)))
        start += kk // (a + 1)
    total = start
    cand, eid = [], []
    for lo in range(0, total, SUBLANES):
        cv = ev = None
        for a, st, nb in segs:
            if st + nb <= lo or st >= lo + SUBLANES:
                continue
            if st <= lo and nb > SUBLANES:
                src_s, src_i = s2[lo - st:lo - st + SUBLANES], i2[lo - st:lo - st + SUBLANES]
            else:
                amt = (st - lo) % SUBLANES
                src_s = pltpu.roll(s2[:SUBLANES], amt, axis=0) if amt else s2[:SUBLANES]
                src_i = pltpu.roll(i2[:SUBLANES], amt, axis=0) if amt else i2[:SUBLANES]
            val = s1[a:a + 1] + src_s
            idv = i1[a:a + 1] * PEER_KEYS + src_i
            p0 = max(st - lo, 0)
            cv = val if p0 == 0 else jnp.where(sub >= p0, val, cv)
            ev = idv if p0 == 0 else jnp.where(sub >= p0, idv, ev)
        if total - lo < SUBLANES:
            cv = jnp.where(sub >= total - lo, -jnp.inf, cv)
        cand.append(cv)
        eid.append(ev)
    return jnp.concatenate(cand, axis=0), jnp.concatenate(eid, axis=0)


def _route_kernel(y_ref, wq_ref, sk_ref, idx_ref, gate_ref):
    yb = y_ref[...].astype(BF16)
    tm = yb.shape[0]
    nt = (((1,), (1,)), ((), ()))
    kk = PEER_TOPK
    sub = lax.broadcasted_iota(jnp.int32, (SUBLANES, tm), 0)

    def head(h, carry):
        q = jnp.dot(yb, wq_ref[h], preferred_element_type=F32).astype(BF16)
        top = []
        for part in range(2):
            qp = q[:, part * PEER_HALF:(part + 1) * PEER_HALF]
            sT = lax.dot_general(sk_ref[2 * h + part], qp, nt, preferred_element_type=F32)
            top.append(_top_rows(sT, kk))
        (s1, i1), (s2, i2) = top
        cand, eid = _packed_candidates(s1, i1, s2, i2, sub)
        sc, ids = _top_rows(cand, kk, payload=eid)
        e = jnp.exp(sc - sc[0:1])
        gate = e / jnp.sum(e, axis=0, keepdims=True)
        r0 = pl.multiple_of(h * kk, kk)
        idx_ref[pl.ds(r0, kk), :] = ids
        gate_ref[pl.ds(r0, kk), :] = gate
        return carry

    lax.fori_loop(0, PEER_HEADS, head, 0)


def _route(h1, wq3, sk, *, tm):
    T, D = h1.shape
    return pl.pallas_call(
        _route_kernel,
        grid=(T // tm,),
        in_specs=[pl.BlockSpec((tm, D), lambda i: (i, 0)), _full(wq3.shape), _full(sk.shape)],
        out_specs=[pl.BlockSpec((PEER_SLOTS, tm), lambda i: (0, i)),
                   pl.BlockSpec((PEER_SLOTS, tm), lambda i: (0, i))],
        out_shape=[jax.ShapeDtypeStruct((PEER_SLOTS, T), jnp.int32),
                   jax.ShapeDtypeStruct((PEER_SLOTS, T), F32)],
        compiler_params=_cparams(("arbitrary",)),
        name="peer_route",
    )(h1, wq3, sk)


def _pack_table(t):
    n, dd = t.shape
    bits = lax.bitcast_convert_type(t, jnp.uint32)
    b16 = (bits + jnp.uint32(0x7FFF) + ((bits >> 16) & jnp.uint32(1))) >> 16
    return (b16[:, :dd // 2] | (b16[:, dd // 2:] << 16)).reshape(n * (dd // (2 * LANES)), LANES)


def _unpack_words(w):
    lo = lax.bitcast_convert_type(w << 16, F32)
    hi = lax.bitcast_convert_type(w & jnp.uint32(0xFFFF0000), F32)
    return lo, hi


def _gather_rows(idx_ref, tab_ref, rs, c, dst):
    irow = idx_ref.at[c]
    for e in range(PEER_SLOTS):
        r0 = pl.multiple_of(irow[e], rs)
        dst[pl.ds(e, rs, stride=CHUNK_STRIDE), :] = tab_ref[pl.ds(r0, rs), :]


def _token_pipeline(tb, gather, fold, ca_ref, cb_ref):
    gather(0, ca_ref)

    def pair(kp, carry):
        c0 = 2 * kp
        gather(c0 + 1, cb_ref)
        carry = fold(c0, ca_ref, carry)
        gather(jnp.minimum(c0 + 2, tb - 1), ca_ref)
        return fold(c0 + 1, cb_ref, carry)

    return pair


def _score_kernel(idx_ref, y_ref, gate_ref, tab_ref, w_ref, ca_ref, cb_ref, *, tb):
    nsl = PEER_SLOTS
    rs = y_ref.shape[1] // 2
    st = CHUNK_STRIDE
    half = rs * LANES
    lane = lax.broadcasted_iota(jnp.int32, (nsl, tb), 1)

    def place(c, tot):
        w_ref[...] = jnp.where(lane == c, jnp.sum(tot, axis=1, keepdims=True), w_ref[...])

    def fold(c, src, pend):
        place(c - 1, pend)
        tot = None
        yv = y_ref[c]
        for s in range(rs):
            lo, hi = _unpack_words(src[s * st:s * st + nsl, :])
            part = lo * yv[s:s + 1] + hi * yv[rs + s:rs + s + 1]
            tot = part if tot is None else tot + part
        return tot

    w_ref[...] = jnp.zeros(w_ref.shape, F32)
    gather = functools.partial(_gather_rows, idx_ref, tab_ref, rs)
    pair = _token_pipeline(tb, gather, fold, ca_ref, cb_ref)
    pend = lax.fori_loop(0, tb // 2, pair, jnp.zeros((nsl, LANES), F32))
    place(tb - 1, pend)
    act = w_ref[...]
    gelu = 0.5 * act * (1.0 + lax.erf(act * (2.0 ** -0.5)))
    w_ref[...] = gate_ref[...] * gelu


def _table_spec(tab):
    return pl.BlockSpec(tab.shape, lambda i: (0, 0), pipeline_mode=pl.Buffered(1))


def _chunk_scratch(rs):
    return pltpu.VMEM((CHUNK_STRIDE * rs, LANES), jnp.uint32)


def _expert_scores(idx, y3, gateT, utab, *, tb):
    T = y3.shape[0]
    kern = functools.partial(_score_kernel, tb=tb)
    return pl.pallas_call(
        kern,
        grid=(T // tb,),
        in_specs=[
            pl.BlockSpec((tb, PEER_SLOTS), lambda i: (i, 0), memory_space=pltpu.SMEM),
            pl.BlockSpec((tb,) + y3.shape[1:], lambda i: (i, 0, 0)),
            pl.BlockSpec((PEER_SLOTS, tb), lambda i: (0, i)),
            _table_spec(utab),
        ],
        out_specs=pl.BlockSpec((PEER_SLOTS, tb), lambda i: (0, i)),
        out_shape=jax.ShapeDtypeStruct((PEER_SLOTS, T), F32),
        scratch_shapes=[_chunk_scratch(y3.shape[1] // 2)] * 2,
        compiler_params=_cparams(("arbitrary",)),
        name="peer_scores",
    )(idx, y3, gateT, utab)


def _mix_kernel(idx_ref, w_ref, tab_ref, f_ref, ca_ref, cb_ref, *, tb):
    nsl = PEER_SLOTS
    rs = f_ref.shape[1] // 2
    st = CHUNK_STRIDE
    half = rs * LANES
    lane = lax.broadcasted_iota(jnp.int32, (nsl, tb), 1)

    def fold(c, src, carry):
        wcol = jnp.sum(jnp.where(lane == c, w_ref[...], 0.0), axis=1, keepdims=True)
        wb = jnp.broadcast_to(wcol, (nsl, LANES))
        los, his = [], []
        for s in range(rs):
            lo, hi = _unpack_words(src[s * st:s * st + nsl, :])
            los.append(jnp.sum(lo * wb, axis=0, keepdims=True))
            his.append(jnp.sum(hi * wb, axis=0, keepdims=True))
        f_ref[c] = jnp.concatenate(los + his, axis=0)
        return carry

    gather = functools.partial(_gather_rows, idx_ref, tab_ref, rs)
    pair = _token_pipeline(tb, gather, fold, ca_ref, cb_ref)
    lax.fori_loop(0, tb // 2, pair, 0)


def _expert_mix(idx, wT, vtab, *, tb, rs):
    T = idx.shape[0]
    rows = 2 * rs
    kern = functools.partial(_mix_kernel, tb=tb)
    return pl.pallas_call(
        kern,
        grid=(T // tb,),
        in_specs=[
            pl.BlockSpec((tb, PEER_SLOTS), lambda i: (i, 0), memory_space=pltpu.SMEM),
            pl.BlockSpec((PEER_SLOTS, tb), lambda i: (0, i)),
            _table_spec(vtab),
        ],
        out_specs=pl.BlockSpec((tb, rows, LANES), lambda i: (i, 0, 0)),
        out_shape=jax.ShapeDtypeStruct((T, rows, LANES), F32),
        scratch_shapes=[_chunk_scratch(rs)] * 2,
        compiler_params=_cparams(("arbitrary",)),
        name="peer_mix",
    )(idx, wT, vtab)


def _ln2_kernel(h_ref, f_ref, g_ref, b_ref, o_ref, *, dn_alpha):
    o_ref[...] = _layer_norm_rows(dn_alpha * h_ref[...] + f_ref[...], g_ref[...], b_ref[...])


def _residual_ln(h1, ffn, g, b, *, tm, dn_alpha):
    T, D = h1.shape
    return pl.pallas_call(
        functools.partial(_ln2_kernel, dn_alpha=dn_alpha),
        grid=(T // tm,),
        in_specs=[pl.BlockSpec((tm, D), lambda i: (i, 0)), pl.BlockSpec((tm, D), lambda i: (i, 0)),
                  _full(g.shape), _full(b.shape)],
        out_specs=pl.BlockSpec((tm, D), lambda i: (i, 0)),
        out_shape=jax.ShapeDtypeStruct((T, D), F32),
        compiler_params=_cparams(("arbitrary",)),
        name="residual_ln2",
    )(h1, ffn, g, b)


def _layer(h2d, B, S, depth, lam_init, w_in, b_gate, lq1, lk1, lq2, lk2, subln_g, w_attn_proj, pool_w,
           pool_scale, w_pool_proj, w_out, ln1_g, ln1_b, peer_wq, peer_subkeys, peer_u, peer_v,
           ln2_g, ln2_b):
    T, D = h2d.shape
    dn_alpha = (2.0 * depth) ** 0.25
    wqk = 2 * N_HEADS * HEAD_DIM
    wv = N_HEADS * V_HEAD_DIM
    wpool = len(POOL_WINDOWS) * POOL_GROUP_DIM
    tm = min(PROJ_ROWS, S)
    ta = min(ATTN_BLOCK, S // 2)
    c0, c1, c2, c3 = wqk, 2 * wqk, 2 * wqk + wv, 2 * wqk + wv + wpool
    wb = w_in.astype(BF16)
    row = lambda a: a.reshape(1, -1).astype(F32)

    qT, k, vTb, p, gates = _project(
        h2d, wb[:, :c0].T, wb[:, c0:c1], wb[:, c1:c2].T, wb[:, c2:c3], wb[:, c3:], row(b_gate),
        tm=tm, ta=ta, scale=HEAD_DIM ** -0.5 * LOG2E)

    slopes = jnp.asarray(np.array([2.0 ** (-8.0 * (h + 1) / N_HEADS) * LOG2E for h in range(N_HEADS)],
                                  dtype=np.float32))
    lamp = jnp.stack([lq1, lk1, lq2, lk2]).astype(F32)
    o = _attention(slopes, lamp, row(subln_g), qT, k, vTb, B=B, S=S, tq=ta, tk=ta, lam_init=lam_init)

    h1, h1_tiles = _merge(o, p, gates, h2d, w_attn_proj.astype(BF16), pool_w.astype(BF16), row(pool_scale),
                w_pool_proj.astype(BF16), w_out.astype(BF16), row(ln1_g), row(ln1_b),
                tm=tm, seq=S, dn_alpha=dn_alpha)

    wq3 = peer_wq.astype(BF16).reshape(D, PEER_HEADS, 2 * PEER_HALF).transpose(1, 0, 2)
    sk = peer_subkeys.astype(BF16).reshape(PEER_HEADS * 2, PEER_KEYS, PEER_HALF)
    idxT, gateT = _route(h1, wq3, sk, tm=min(ROUTE_ROWS, T))

    tb = min(PEER_ROWS, T)
    rs = D // (2 * LANES)
    idx = idxT.T * rs
    wT = _expert_scores(idx, h1_tiles, gateT, _pack_table(peer_u), tb=tb)
    ffn = _expert_mix(idx, wT, _pack_table(peer_v), tb=tb, rs=rs)
    return _residual_ln(h1, ffn.reshape(T, D), row(ln2_g), row(ln2_b), tm=tm, dn_alpha=dn_alpha)


def kernel(x, w_in, b_gate, lambda_q1, lambda_k1, lambda_q2, lambda_k2, subln_g, w_attn_proj, pool_w,
           pool_scale, w_pool_proj, w_out, ln1_g, ln1_b, peer_wq, peer_subkeys, peer_u, peer_v,
           ln2_g, ln2_b):
    B, S, D = x.shape
    depth = w_in.shape[0]
    h = x.reshape(B * S, D)
    for l in range(depth):
        lam_init = 0.8 - 0.6 * math.exp(-0.3 * l)
        h = _layer(h, B, S, depth, lam_init, w_in[l], b_gate[l], lambda_q1[l], lambda_k1[l],
                   lambda_q2[l], lambda_k2[l], subln_g[l], w_attn_proj[l], pool_w[l], pool_scale[l],
                   w_pool_proj[l], w_out[l], ln1_g[l], ln1_b[l], peer_wq[l], peer_subkeys[l],
                   peer_u[l], peer_v[l], ln2_g[l], ln2_b[l])
    return h.reshape(B, S, D)
```
